```python
import jax, jax.numpy as jnp
from jax import lax
import numpy as np

D_MODEL = 1024
BATCH = 4
SEQ = 8192
DEPTH = 4

N_MIXERS = 2
N_RG_LAYERS = (DEPTH + 1) // 2
N_NSA_LAYERS = DEPTH // 2
RMS_EPS = 1e-6
D_FF = 4 * D_MODEL
D_RNN = D_MODEL
RG_BLOCKS = 4
RG_BLOCK_W = D_RNN // RG_BLOCKS
CONV_W = 4
RG_C = 8.0
N_HEADS = 16
HEAD_DIM = D_MODEL // N_HEADS
N_KV_GROUPS = 4
HEADS_PER_GROUP = N_HEADS // N_KV_GROUPS
CMP_LEN = 32
CMP_STRIDE = 16
CMP_RATIO = CMP_LEN // CMP_STRIDE
SLC_LEN = 64
N_SELECT = 16
WINDOW = 512
Q_BLOCK = 128
Q_COLS = N_HEADS * HEAD_DIM
KV_COLS = N_KV_GROUPS * HEAD_DIM
GATE_COLS = 3 * N_HEADS
NSA_IN_COLS = Q_COLS + 6 * KV_COLS + GATE_COLS
ALIBI_MAX = 8.0
NEG_INF = -1e30
FORCE_BONUS = 1e4

kernel_name = 'hybrid_rglru_nsa_trunk'


def _rmsnorm(x, g):
    xf = x.astype(jnp.float32)
    y = xf * lax.rsqrt(jnp.mean(xf * xf, axis=-1, keepdims=True) + RMS_EPS)
    return (y * g.astype(jnp.float32)).astype(x.dtype)


def _masked_softmax(s, mask):
    s = jnp.where(mask, s, NEG_INF)
    m = jnp.max(s, axis=-1, keepdims=True)
    p = jnp.where(mask, jnp.exp(s - m), 0.0)
    return p / jnp.maximum(jnp.sum(p, axis=-1, keepdims=True), 1e-30)


def _alibi_slopes():
    h = jnp.arange(1, N_HEADS + 1, dtype=jnp.float32)
    return jnp.exp2(-ALIBI_MAX * h / N_HEADS).reshape(N_KV_GROUPS, HEADS_PER_GROUP)


def _sq_relu_mlp(h, w_up, w_down):
    u = jax.nn.relu(h @ w_up)
    return (u * u) @ w_down


def _causal_depthwise_conv(x, w, b):
    y = lax.conv_general_dilated(
        x, w[:, None, :].astype(x.dtype), window_strides=(1,),
        padding=[(CONV_W - 1, 0)], dimension_numbers=('NWC', 'WIO', 'NWC'),
        feature_group_count=x.shape[-1])
    return y + b


def _lru_combine(left, right):
    a_l, b_l = left
    a_r, b_r = right
    return a_l * a_r, a_r * b_l + b_r


def _rglru_mixer(h, w_in, conv_w, conv_b, w_a, b_a, w_x, b_x, lam, w_out):
    B, S, _ = h.shape
    proj = h @ w_in
    y = jax.nn.gelu(proj[..., :D_RNN], approximate=True)
    xb = _causal_depthwise_conv(proj[..., D_RNN:], conv_w, conv_b)
    xr = xb.reshape(B, S, RG_BLOCKS, RG_BLOCK_W)
    r = jax.nn.sigmoid(jnp.einsum('bsnc,ncd->bsnd', xr, w_a).reshape(B, S, D_RNN) + b_a)
    i = jax.nn.sigmoid(jnp.einsum('bsnc,ncd->bsnd', xr, w_x).reshape(B, S, D_RNN) + b_x)
    log_a = -RG_C * r.astype(jnp.float32) * jax.nn.softplus(-lam.astype(jnp.float32))
    a = jnp.exp(log_a)
    u = jnp.sqrt(-jnp.expm1(2.0 * log_a)) * (i * xb).astype(jnp.float32)
    _, hs = lax.associative_scan(_lru_combine, (a, u), axis=1)
    return (hs.astype(h.dtype) * y) @ w_out


def _compress(z, pe, w1, w2):
    B, S, G, DH = z.shape
    n_chunk = S // CMP_STRIDE
    nc = n_chunk - CMP_RATIO + 1
    chunks = z.reshape(B, n_chunk, CMP_STRIDE, G, DH)
    blocks = jnp.concatenate([chunks[:, r:r + nc] for r in range(CMP_RATIO)], axis=2)
    blocks = blocks + pe[None, None, :, None, :]
    flat = jnp.transpose(blocks, (0, 1, 3, 2, 4)).reshape(B, nc, G, CMP_LEN * DH)
    return jax.nn.gelu(flat @ w1, approximate=True) @ w2


def _cmp_to_slc_map(nc, nb):
    cs = jnp.arange(nc) * CMP_STRIDE
    ss = jnp.arange(nb) * SLC_LEN
    ov = jnp.minimum(cs[:, None] + CMP_LEN, ss[None, :] + SLC_LEN) - jnp.maximum(cs[:, None], ss[None, :])
    return (jnp.maximum(ov, 0) / CMP_STRIDE).astype(jnp.float32)


def _nsa_attend(q, gates, k_c, v_c, k_s, v_s, k_w, v_w):
    B, S = q.shape[0], q.shape[1]
    nqb = S // Q_BLOCK
    nc = k_c.shape[1]
    nb = k_s.shape[2]
    n_sel = min(N_SELECT, nb)
    G, Hg, DH = N_KV_GROUPS, HEADS_PER_GROUP, HEAD_DIM
    qx = q.reshape(B * nqb, Q_BLOCK, G, Hg, DH)
    gx = gates.reshape(B * nqb, Q_BLOCK, G, Hg, 3)
    slopes = _alibi_slopes()[:, :, None, None]
    cmp_map = _cmp_to_slc_map(nc, nb)
    cmp_end = jnp.arange(nc) * CMP_STRIDE + (CMP_LEN - 1)
    blk_ids = jnp.arange(nb)
    key_off = jnp.arange(SLC_LEN)
    win_off = jnp.arange(WINDOW + Q_BLOCK)
    gi = jnp.arange(G)[:, None, None]
    scale = HEAD_DIM ** -0.5
    zero = jnp.zeros((), jnp.int32)

    def step(args):
        idx, q_blk, g_blk = args
        b = idx // nqb
        qb = idx % nqb
        t = qb * Q_BLOCK + jnp.arange(Q_BLOCK)
        qf = q_blk.astype(jnp.float32) * scale

        kc = lax.dynamic_index_in_dim(k_c, b, 0, keepdims=False).astype(jnp.float32)
        vc = lax.dynamic_index_in_dim(v_c, b, 0, keepdims=False).astype(jnp.float32)
        d_c = t[:, None] - cmp_end[None, :]
        s_c = jnp.einsum('qghd,ngd->ghqn', qf, kc) - slopes * d_c.astype(jnp.float32)
        p_c = _masked_softmax(s_c, d_c >= 0)
        o_c = jnp.einsum('ghqn,ngd->qghd', p_c, vc)

        imp = jnp.einsum('ghqn,nj->gqj', p_c, cmp_map)
        cur = t // SLC_LEN
        forced = (blk_ids[None, :] == 0) | (blk_ids[None, :] == cur[:, None]) | (blk_ids[None, :] == cur[:, None] - 1)
        score = jnp.where(blk_ids[None, :] <= cur[:, None], imp + FORCE_BONUS * forced, -1.0)
        sel = lax.top_k(score, n_sel)[1]
        sel_ok = sel <= cur[None, :, None]

        ks = lax.dynamic_index_in_dim(k_s, b, 0, keepdims=False)
        vs = lax.dynamic_index_in_dim(v_s, b, 0, keepdims=False)
        k_sel = ks[gi, sel].astype(jnp.float32).reshape(G, Q_BLOCK, n_sel * SLC_LEN, DH)
        v_sel = vs[gi, sel].astype(jnp.float32).reshape(G, Q_BLOCK, n_sel * SLC_LEN, DH)
        pos = (sel[..., None] * SLC_LEN + key_off).reshape(G, Q_BLOCK, n_sel * SLC_LEN)
        d_s = t[None, :, None] - pos
        m_s = (d_s >= 0) & jnp.repeat(sel_ok, SLC_LEN, axis=-1)
        s_s = jnp.einsum('qghd,gqkd->ghqk', qf, k_sel) - slopes * d_s[:, None].astype(jnp.float32)
        p_s = _masked_softmax(s_s, m_s[:, None])
        o_s = jnp.einsum('ghqk,gqkd->qghd', p_s, v_sel)

        kw = lax.dynamic_slice(k_w, (b, qb * Q_BLOCK, zero, zero), (1, WINDOW + Q_BLOCK, G, DH))[0].astype(jnp.float32)
        vw = lax.dynamic_slice(v_w, (b, qb * Q_BLOCK, zero, zero), (1, WINDOW + Q_BLOCK, G, DH))[0].astype(jnp.float32)
        s_pos = qb * Q_BLOCK - WINDOW + win_off
        d_w = t[:, None] - s_pos[None, :]
        m_w = (d_w >= 0) & (d_w < WINDOW) & (s_pos[None, :] >= 0)
        s_w = jnp.einsum('qghd,kgd->ghqk', qf, kw) - slopes * d_w.astype(jnp.float32)
        p_w = _masked_softmax(s_w, m_w)
        o_w = jnp.einsum('ghqk,kgd->qghd', p_w, vw)

        g = g_blk.astype(jnp.float32)
        o = g[..., 0:1] * o_c + g[..., 1:2] * o_s + g[..., 2:3] * o_w
        return o.reshape(Q_BLOCK, N_HEADS * HEAD_DIM)

    out = lax.map(step, (jnp.arange(B * nqb, dtype=jnp.int32), qx, gx))
    return out.reshape(B, S, N_HEADS * HEAD_DIM)


def _nsa_mixer(h, w_in, b_gate, pe_k, pe_v, w1_k, w2_k, w1_v, w2_v, w_out):
    B, S, _ = h.shape
    proj = h @ w_in
    q = proj[..., :Q_COLS].reshape(B, S, N_KV_GROUPS, HEADS_PER_GROUP, HEAD_DIM)
    kv = proj[..., Q_COLS:Q_COLS + 6 * KV_COLS].reshape(B, S, 6, N_KV_GROUPS, HEAD_DIM)
    gates = jax.nn.sigmoid(proj[..., Q_COLS + 6 * KV_COLS:] + b_gate).reshape(B, S, N_HEADS, 3)
    k_c = _compress(kv[:, :, 0], pe_k, w1_k, w2_k)
    v_c = _compress(kv[:, :, 1], pe_v, w1_v, w2_v)
    nb = S // SLC_LEN
    k_s = jnp.transpose(kv[:, :, 2].reshape(B, nb, SLC_LEN, N_KV_GROUPS, HEAD_DIM), (0, 3, 1, 2, 4))
    v_s = jnp.transpose(kv[:, :, 3].reshape(B, nb, SLC_LEN, N_KV_GROUPS, HEAD_DIM), (0, 3, 1, 2, 4))
    pad = ((0, 0), (WINDOW, 0), (0, 0), (0, 0))
    k_w = jnp.pad(kv[:, :, 4], pad)
    v_w = jnp.pad(kv[:, :, 5], pad)
    o = _nsa_attend(q, gates, k_c, v_c, k_s, v_s, k_w, v_w)
    return o.astype(h.dtype) @ w_out


def setup_inputs(seed: int = 0) -> dict:
    key = jax.random.key(seed)
    ks = jax.random.split(key, 24)
    f32 = jnp.float32
    nrm = lambda k, shape, s: jax.random.normal(k, shape, f32) * s
    u = jax.random.uniform(ks[11], (N_RG_LAYERS, D_RNN), f32, minval=0.9, maxval=0.999)
    sig = u ** (1.0 / RG_C)
    return {
        'x': nrm(ks[0], (BATCH, SEQ, D_MODEL), 1.0),
        'norm_mix': 1.0 + nrm(ks[1], (DEPTH, D_MODEL), 0.02),
        'norm_ffn': 1.0 + nrm(ks[2], (DEPTH, D_MODEL), 0.02),
        'norm_final': 1.0 + nrm(ks[3], (D_MODEL,), 0.02),
        'rg_w_in': nrm(ks[4], (N_RG_LAYERS, D_MODEL, 2 * D_RNN), D_MODEL ** -0.5),
        'rg_conv_w': nrm(ks[5], (N_RG_LAYERS, CONV_W, D_RNN), CONV_W ** -0.5),
        'rg_conv_b': nrm(ks[6], (N_RG_LAYERS, D_RNN), 0.01),
        'rg_w_a': nrm(ks[7], (N_RG_LAYERS, RG_BLOCKS, RG_BLOCK_W, RG_BLOCK_W), RG_BLOCK_W ** -0.5),
        'rg_b_a': nrm(ks[8], (N_RG_LAYERS, D_RNN), 0.01),
        'rg_w_x': nrm(ks[9], (N_RG_LAYERS, RG_BLOCKS, RG_BLOCK_W, RG_BLOCK_W), RG_BLOCK_W ** -0.5),
        'rg_b_x': nrm(ks[10], (N_RG_LAYERS, D_RNN), 0.01),
        'rg_lambda': jnp.log(sig) - jnp.log1p(-sig),
        'rg_w_out': nrm(ks[12], (N_RG_LAYERS, D_RNN, D_MODEL), D_RNN ** -0.5),
        'nsa_w_in': nrm(ks[13], (N_NSA_LAYERS, D_MODEL, NSA_IN_COLS), D_MODEL ** -0.5),
        'nsa_b_gate': nrm(ks[14], (N_NSA_LAYERS, GATE_COLS), 0.01),
        'nsa_pe_k': nrm(ks[15], (N_NSA_LAYERS, CMP_LEN, HEAD_DIM), 0.02),
        'nsa_pe_v': nrm(ks[16], (N_NSA_LAYERS, CMP_LEN, HEAD_DIM), 0.02),
        'nsa_w1_k': nrm(ks[17], (N_NSA_LAYERS, CMP_LEN * HEAD_DIM, HEAD_DIM), (CMP_LEN * HEAD_DIM) ** -0.5),
        'nsa_w2_k': nrm(ks[18], (N_NSA_LAYERS, HEAD_DIM, HEAD_DIM), HEAD_DIM ** -0.5),
        'nsa_w1_v': nrm(ks[19], (N_NSA_LAYERS, CMP_LEN * HEAD_DIM, HEAD_DIM), (CMP_LEN * HEAD_DIM) ** -0.5),
        'nsa_w2_v': nrm(ks[20], (N_NSA_LAYERS, HEAD_DIM, HEAD_DIM), HEAD_DIM ** -0.5),
        'nsa_w_out': nrm(ks[21], (N_NSA_LAYERS, N_HEADS * HEAD_DIM, D_MODEL), (N_HEADS * HEAD_DIM) ** -0.5),
        'mlp_w_up': nrm(ks[22], (DEPTH, D_MODEL, D_FF), D_MODEL ** -0.5),
        'mlp_w_down': nrm(ks[23], (DEPTH, D_FF, D_MODEL), D_FF ** -0.5),
    }


def reference(x, norm_mix, norm_ffn, norm_final,
              rg_w_in, rg_conv_w, rg_conv_b, rg_w_a, rg_b_a, rg_w_x, rg_b_x, rg_lambda, rg_w_out,
              nsa_w_in, nsa_b_gate, nsa_pe_k, nsa_pe_v, nsa_w1_k, nsa_w2_k, nsa_w1_v, nsa_w2_v, nsa_w_out,
              mlp_w_up, mlp_w_down):
    for i in range(DEPTH):
        h = _rmsnorm(x, norm_mix[i])
        j = i // N_MIXERS
        if i % N_MIXERS == 0:
            x = x + _rglru_mixer(h, rg_w_in[j], rg_conv_w[j], rg_conv_b[j], rg_w_a[j], rg_b_a[j],
                                 rg_w_x[j], rg_b_x[j], rg_lambda[j], rg_w_out[j])
        else:
            x = x + _nsa_mixer(h, nsa_w_in[j], nsa_b_gate[j], nsa_pe_k[j], nsa_pe_v[j],
                               nsa_w1_k[j], nsa_w2_k[j], nsa_w1_v[j], nsa_w2_v[j], nsa_w_out[j])
        h = _rmsnorm(x, norm_ffn[i])
        x = x + _sq_relu_mlp(h, mlp_w_up[i], mlp_w_down[i])
    return _rmsnorm(x, norm_final)
```

```python
import functools

import jax
import jax.numpy as jnp
from jax import lax
from jax.experimental import pallas as pl
from jax.experimental.pallas import tpu as pltpu

F32 = jnp.float32
BF16 = jnp.bfloat16

RMS_EPS = 1e-6
RG_BLOCKS = 4
CONV_W = 4
RG_C = 8.0
N_HEADS = 16
HEAD_DIM = 64
N_KV_GROUPS = 4
HEADS_PER_GROUP = N_HEADS // N_KV_GROUPS
CMP_LEN = 32
CMP_STRIDE = 16
SLC_LEN = 64
N_SELECT = 16
WINDOW = 512
ALIBI_MAX = 8.0
NEG_INF = -1e30
FORCE_BONUS = 1e4

VMEM_LIMIT_BYTES = 56 * 1024 * 1024
LANES = 128
CMP_GROUPS = LANES // HEAD_DIM

ROW_TILE = 512
SCAN_TILE = 256
Q_TILE = 128
KEY_TILE = 512
FF_CHUNK = 1024


def _cparams(*sem):
    return pltpu.CompilerParams(dimension_semantics=sem, vmem_limit_bytes=VMEM_LIMIT_BYTES)


def _resident(shape):
    nd = len(shape)
    return pl.BlockSpec(shape, lambda *_: (0,) * nd, pipeline_mode=pl.Buffered(1))


def _rmsnorm(x, g):
    y = x * lax.rsqrt(jnp.mean(x * x, axis=-1, keepdims=True) + RMS_EPS)
    return y * g


def _gelu_tanh(x):
    return jax.nn.gelu(x, approximate=True)


def _dot(a, b):
    return jnp.dot(a, b, preferred_element_type=F32)


def _dot_nt(a, b):
    return lax.dot_general(a, b, (((1,), (1,)), ((), ())), preferred_element_type=F32)


def _rg_in_kernel(x_ref, g_ref, w_ref, y_ref, xp_ref):
    d = y_ref.shape[-1]
    h = _rmsnorm(x_ref[...], g_ref[...]).astype(BF16)
    proj = _dot(h, w_ref[...])
    y_ref[...] = _gelu_tanh(proj[:, :d])
    xp_ref[...] = proj[:, d:]


def _rg_in(x2, g, w_in):
    n, d = x2.shape
    return pl.pallas_call(
        _rg_in_kernel,
        grid=(n // ROW_TILE,),
        in_specs=[
            pl.BlockSpec((ROW_TILE, d), lambda i: (i, 0)),
            _resident((1, d)),
            _resident(w_in.shape),
        ],
        out_specs=[pl.BlockSpec((ROW_TILE, d), lambda i: (i, 0))] * 2,
        out_shape=[jax.ShapeDtypeStruct((n, d), F32)] * 2,
        compiler_params=_cparams("parallel"),
        name="rg_in",
    )(x2, g, w_in)


def _shift_rows(x, k, fill):
    rows = lax.broadcasted_iota(jnp.int32, x.shape, 0)
    return jnp.where(rows >= k, pltpu.roll(x, k, axis=0), fill)


def _rg_scan_kernel(xp_ref, y_ref, x_ref, cw_ref, cb_ref, wa_ref, ba_ref, wx_ref, bx_ref,
                    lam_ref, wo_ref, o_ref, tail_ref, h_ref):
    ts, d = xp_ref.shape
    bw = d // RG_BLOCKS

    @pl.when(pl.program_id(1) == 0)
    def _():
        tail_ref[...] = jnp.zeros_like(tail_ref)
        h_ref[...] = jnp.zeros_like(h_ref)

    xp = xp_ref[...]
    ext = jnp.concatenate([tail_ref[...], xp], axis=0)
    cw = cw_ref[...]
    xb = cb_ref[...] + cw[CONV_W - 1:CONV_W] * xp
    for k in range(1, CONV_W):
        xb = xb + cw[CONV_W - 1 - k:CONV_W - k] * pltpu.roll(ext, k, axis=0)[8:]
    tail_ref[...] = xp[ts - 8:]

    xb16 = xb.astype(BF16)
    ra = jnp.concatenate(
        [_dot(xb16[:, n * bw:(n + 1) * bw], wa_ref[n]) for n in range(RG_BLOCKS)], axis=1)
    rx = jnp.concatenate(
        [_dot(xb16[:, n * bw:(n + 1) * bw], wx_ref[n]) for n in range(RG_BLOCKS)], axis=1)
    r = jax.nn.sigmoid(ra + ba_ref[...])
    i = jax.nn.sigmoid(rx + bx_ref[...])
    nl = -lam_ref[...]
    softplus = jnp.maximum(nl, 0.0) + jnp.log1p(jnp.exp(-jnp.abs(nl)))
    log_a = -RG_C * r * softplus
    a = jnp.exp(log_a)
    b = jnp.sqrt(-jnp.tanh(log_a) * (a * a + 1.0)) * (i * xb)

    k = 1
    while k < ts:
        b = b + a * _shift_rows(b, k, 0.0)
        a = a * _shift_rows(a, k, 1.0)
        k *= 2
    hs = a * h_ref[0:1, :] + b
    h_ref[...] = jnp.broadcast_to(hs[ts - 1:ts, :], h_ref.shape)

    o_ref[...] = x_ref[...] + _dot((hs * y_ref[...]).astype(BF16), wo_ref[...])


def _rg_scan(xp, y, x, conv_w, conv_b, w_a, b_a, w_x, b_x, lam, w_out):
    bsz, s, d = x.shape
    tile = lambda: pl.BlockSpec((None, SCAN_TILE, d), lambda b, i: (b, i, 0))
    return pl.pallas_call(
        _rg_scan_kernel,
        grid=(bsz, s // SCAN_TILE),
        in_specs=[tile(), tile(), tile(),
                  _resident(conv_w.shape), _resident(conv_b.shape),
                  _resident(w_a.shape), _resident(b_a.shape),
                  _resident(w_x.shape), _resident(b_x.shape),
                  _resident(lam.shape), _resident(w_out.shape)],
        out_specs=tile(),
        out_shape=jax.ShapeDtypeStruct((bsz, s, d), F32),
        scratch_shapes=[pltpu.VMEM((8, d), F32), pltpu.VMEM((8, d), F32)],
        compiler_params=_cparams("parallel", "arbitrary"),
        name="rg_scan",
    )(xp, y, x, conv_w, conv_b, w_a, b_a, w_x, b_x, lam, w_out)


def _nsa_in_kernel(x_ref, g_ref, wq_ref, wc_ref, wa_ref, wg_ref, bg_ref,
                   q_ref, kvc_ref, kva_ref, gate_ref):
    h = _rmsnorm(x_ref[...], g_ref[...]).astype(BF16)
    q_ref[...] = (_dot(h, wq_ref[...]) * (HEAD_DIM ** -0.5)).astype(q_ref.dtype)
    kvc_ref[...] = _dot(h, wc_ref[...])
    kva = _dot(h, wa_ref[...])
    for j in range(kva_ref.shape[0]):
        kva_ref[j] = kva[:, j * HEAD_DIM:(j + 1) * HEAD_DIM].astype(kva_ref.dtype)
    gate_ref[...] = jax.nn.sigmoid(_dot(h, wg_ref[...]) + bg_ref[...])


def _nsa_in(x, g, wq, wc, wa, wg, bg):
    bsz, s, d = x.shape
    nj = wa.shape[1] // HEAD_DIM
    row = lambda w: pl.BlockSpec((None, ROW_TILE, w), lambda b, i: (b, i, 0))
    return pl.pallas_call(
        _nsa_in_kernel,
        grid=(bsz, s // ROW_TILE),
        in_specs=[row(d), _resident(g.shape), _resident(wq.shape), _resident(wc.shape),
                  _resident(wa.shape), _resident(wg.shape), _resident(bg.shape)],
        out_specs=[row(wq.shape[1]), row(wc.shape[1]),
                   pl.BlockSpec((None, nj, ROW_TILE, HEAD_DIM), lambda b, i: (b, 0, i, 0)),
                   row(wg.shape[1])],
        out_shape=[jax.ShapeDtypeStruct((bsz, s, wq.shape[1]), BF16),
                   jax.ShapeDtypeStruct((bsz, s, wc.shape[1]), F32),
                   jax.ShapeDtypeStruct((bsz, nj, s, HEAD_DIM), BF16),
                   jax.ShapeDtypeStruct((bsz, s, wg.shape[1]), F32)],
        compiler_params=_cparams("parallel", "parallel"),
        name="nsa_in",
    )(x, g, wq, wc, wa, wg, bg)


def _compress_kernel(z_ref, pe_ref, w1_ref, w2_ref, o_ref):
    nchunk = o_ref.shape[1]
    width = z_ref.shape[-1]
    acc = [jnp.zeros((nchunk, width), F32) for _ in range(CMP_LEN // CMP_STRIDE)]
    for p in range(CMP_STRIDE):
        zp = z_ref[pl.ds(p, nchunk, stride=CMP_STRIDE), :]
        for r in range(len(acc)):
            pp = r * CMP_STRIDE + p
            acc[r] = acc[r] + _dot((zp + pe_ref[pp:pp + 1, :]).astype(BF16), w1_ref[pp])
    pre = acc[0] + pltpu.roll(acc[1], nchunk - 1, axis=0)
    out = _dot(_gelu_tanh(pre).astype(BF16), w2_ref[...])
    for gi in range(o_ref.shape[0]):
        o_ref[gi] = out[:, gi * HEAD_DIM:(gi + 1) * HEAD_DIM].astype(o_ref.dtype)


def _compress(kvc, pe, w1, w2):
    bsz, s, _ = kvc.shape
    nchunk = s // CMP_STRIDE
    halves = N_KV_GROUPS // CMP_GROUPS
    return pl.pallas_call(
        _compress_kernel,
        grid=(bsz, 2, halves),
        in_specs=[
            pl.BlockSpec((None, s, LANES), lambda b, t, c: (b, 0, t * halves + c)),
            pl.BlockSpec((None, CMP_LEN, LANES), lambda b, t, c: (t, 0, 0)),
            pl.BlockSpec((None, CMP_LEN, LANES, LANES), lambda b, t, c: (t, 0, 0, 0)),
            pl.BlockSpec((None, LANES, LANES), lambda b, t, c: (t, 0, 0)),
        ],
        out_specs=pl.BlockSpec((None, None, CMP_GROUPS, nchunk, HEAD_DIM),
                               lambda b, t, c: (b, t, c, 0, 0)),
        out_shape=jax.ShapeDtypeStruct((bsz, 2, N_KV_GROUPS, nchunk, HEAD_DIM), BF16),
        compiler_params=_cparams("parallel", "parallel", "parallel"),
        name="nsa_compress",
    )(kvc, pe, w1, w2)


def _masked_softmax(s, mask):
    s = jnp.where(mask, s, NEG_INF)
    m = jnp.max(s, axis=-1, keepdims=True)
    p = jnp.where(mask, jnp.exp(s - m), 0.0)
    return p / jnp.maximum(jnp.sum(p, axis=-1, keepdims=True), 1e-30)


def _split3(x):
    hi = x.astype(BF16)
    r = x - hi.astype(F32)
    mid = r.astype(BF16)
    lo = (r - mid.astype(F32)).astype(BF16)
    return hi, mid, lo


def _attn_kernel(slope_ref, q_ref, gate_ref, kc_ref, vc_ref, ks_ref, vs_ref, kw_ref, vw_ref,
                 cmap_ref, o_ref, *, n_sel):
    hg, dh, tq = HEADS_PER_GROUP, HEAD_DIM, Q_TILE
    g = pl.program_id(1)
    t0 = pl.program_id(2) * tq
    slopes = [slope_ref[g * hg + h] for h in range(hg)]
    q = q_ref[...]
    qs = jnp.concatenate([q[:, h * dh:(h + 1) * dh] for h in range(hg)], axis=0)
    t = t0 + lax.broadcasted_iota(jnp.int32, (tq, 1), 0)

    def heads(fn):
        return [fn(h, slice(h * tq, (h + 1) * tq)) for h in range(hg)]

    nc = kc_ref.shape[0]
    s_c = _dot_nt(qs, kc_ref[...])
    cmp_end = lax.broadcasted_iota(jnp.int32, (1, nc), 1) * CMP_STRIDE + (CMP_LEN - 1)
    d_c = t - cmp_end
    m_c = d_c >= 0
    d_cf = d_c.astype(F32)
    p_c = heads(lambda h, rows: _masked_softmax(s_c[rows] - slopes[h] * d_cf, m_c))
    o_c = _dot(jnp.concatenate(p_c, axis=0).astype(BF16), vc_ref[...])

    nb = cmap_ref.shape[0]
    p_sum = p_c[0]
    for h in range(1, hg):
        p_sum = p_sum + p_c[h]
    cmap = cmap_ref[...]
    imp = sum(_dot_nt(cmap, part) for part in _split3(p_sum))
    blk = lax.broadcasted_iota(jnp.int32, (nb, tq), 0)
    cur = (t0 + lax.broadcasted_iota(jnp.int32, (1, tq), 1)) // SLC_LEN
    forced = (blk == 0) | (blk == cur) | (blk == cur - 1)
    visible = blk <= cur
    score = jnp.where(visible, imp + FORCE_BONUS * forced.astype(F32), -1.0)
    sel = jnp.zeros((nb, tq), F32)
    blk_f = blk.astype(F32)
    for _ in range(n_sel):
        best = jnp.max(score, axis=0, keepdims=True)
        first = jnp.min(jnp.where(score == best, blk_f, float(nb)), axis=0, keepdims=True)
        pick = blk_f == first
        sel = jnp.where(pick, 1.0, sel)
        score = jnp.where(pick, -jnp.inf, score)
    sel = jnp.where(visible, sel, 0.0).T.astype(BF16)

    tk = KEY_TILE
    col_blk = lax.broadcasted_iota(jnp.int32, (nb, tk), 1) // SLC_LEN
    row_blk = lax.broadcasted_iota(jnp.int32, (nb, tk), 0)
    col_pos = lax.broadcasted_iota(jnp.int32, (1, tk), 1)

    def sel_step(kt, carry):
        m_prev, l_prev, acc = carry
        k0 = pl.multiple_of(kt * tk, tk)
        s_s = _dot_nt(qs, ks_ref[pl.ds(k0, tk), :])
        expand = (row_blk - kt * (tk // SLC_LEN) == col_blk).astype(BF16)
        chosen = _dot(sel, expand) > 0.5
        d_s = t - (k0 + col_pos)
        mask = chosen & (d_s >= 0)
        d_sf = d_s.astype(F32)
        m_new, l_new, p_all, alpha = [], [], [], []
        for h in range(hg):
            rows = slice(h * tq, (h + 1) * tq)
            sm = jnp.where(mask, s_s[rows] - slopes[h] * d_sf, NEG_INF)
            mn = jnp.maximum(m_prev[rows], jnp.max(sm, axis=-1, keepdims=True))
            p = jnp.where(mask, jnp.exp(sm - mn), 0.0)
            al = jnp.exp(m_prev[rows] - mn)
            m_new.append(mn)
            alpha.append(al)
            l_new.append(al * l_prev[rows] + jnp.sum(p, axis=-1, keepdims=True))
            p_all.append(p)
        pv = _dot(jnp.concatenate(p_all, axis=0).astype(BF16), vs_ref[pl.ds(k0, tk), :])
        return (jnp.concatenate(m_new, axis=0), jnp.concatenate(l_new, axis=0),
                jnp.concatenate(alpha, axis=0) * acc + pv)

    n_tiles = (t0 + tq + tk - 1) // tk
    init = (jnp.full((hg * tq, 1), NEG_INF, F32), jnp.zeros((hg * tq, 1), F32),
            jnp.zeros((hg * tq, dh), F32))
    _, l_s, acc_s = lax.fori_loop(0, n_tiles, sel_step, init)
    o_s = acc_s / jnp.maximum(l_s, 1e-30)

    wk = WINDOW + tq
    w0 = pl.multiple_of(jnp.maximum(t0 - WINDOW, 0), tq)
    s_w = _dot_nt(qs, kw_ref[pl.ds(w0, wk), :])
    d_w = t - (w0 + lax.broadcasted_iota(jnp.int32, (1, wk), 1))
    m_w = (d_w >= 0) & (d_w < WINDOW)
    d_wf = d_w.astype(F32)
    p_w = heads(lambda h, rows: _masked_softmax(s_w[rows] - slopes[h] * d_wf, m_w))
    o_w = _dot(jnp.concatenate(p_w, axis=0).astype(BF16), vw_ref[pl.ds(w0, wk), :])

    gate = gate_ref[...]
    out = []
    for h in range(hg):
        rows = slice(h * tq, (h + 1) * tq)
        c = 3 * h
        out.append(gate[:, c:c + 1] * o_c[rows] + gate[:, c + 1:c + 2] * o_s[rows]
                   + gate[:, c + 2:c + 3] * o_w[rows])
    o_ref[...] = jnp.concatenate(out, axis=1).astype(o_ref.dtype)


def _attend(slopes, q, gates, kvc, kva, cmap):
    bsz, s, _ = q.shape
    gw = HEADS_PER_GROUP * HEAD_DIM
    nchunk = kvc.shape[3]
    nb = s // SLC_LEN
    n_sel = min(N_SELECT, nb)
    cmp_spec = lambda t: pl.BlockSpec((None, None, None, nchunk, HEAD_DIM),
                                      lambda b, g, i, sl: (b, t, g, 0, 0))
    kv_spec = lambda t: pl.BlockSpec((None, None, s, HEAD_DIM),
                                     lambda b, g, i, sl: (b, t * N_KV_GROUPS + g, 0, 0))
    grid_spec = pltpu.PrefetchScalarGridSpec(
        num_scalar_prefetch=1,
        grid=(bsz, N_KV_GROUPS, s // Q_TILE),
        in_specs=[
            pl.BlockSpec((None, Q_TILE, gw), lambda b, g, i, sl: (b, i, g)),
            pl.BlockSpec((None, Q_TILE, 128), lambda b, g, i, sl: (b, i, g)),
            cmp_spec(0), cmp_spec(1),
            kv_spec(0), kv_spec(1), kv_spec(2), kv_spec(3),
            pl.BlockSpec(cmap.shape, lambda b, g, i, sl: (0, 0)),
        ],
        out_specs=pl.BlockSpec((None, Q_TILE, gw), lambda b, g, i, sl: (b, i, g)),
    )
    return pl.pallas_call(
        functools.partial(_attn_kernel, n_sel=n_sel),
        grid_spec=grid_spec,
        out_shape=jax.ShapeDtypeStruct((bsz, s, N_HEADS * HEAD_DIM), BF16),
        compiler_params=_cparams("parallel", "parallel", "arbitrary"),
        name="nsa_attend",
    )(slopes, q, gates, kvc, kvc, kva, kva, kva, kva, cmap)


def _out_proj_kernel(o_ref, w_ref, x_ref, y_ref):
    y_ref[...] = x_ref[...] + _dot(o_ref[...], w_ref[...])


def _out_proj(o2, w, x2):
    n, d = x2.shape
    return pl.pallas_call(
        _out_proj_kernel,
        grid=(n // ROW_TILE,),
        in_specs=[pl.BlockSpec((ROW_TILE, o2.shape[1]), lambda i: (i, 0)), _resident(w.shape),
                  pl.BlockSpec((ROW_TILE, d), lambda i: (i, 0))],
        out_specs=pl.BlockSpec((ROW_TILE, d), lambda i: (i, 0)),
        out_shape=jax.ShapeDtypeStruct((n, d), F32),
        compiler_params=_cparams("parallel"),
        name="nsa_out",
    )(o2, w, x2)


def _mlp_kernel(x_ref, g_ref, wu_ref, wd_ref, gf_ref, o_ref, *, final_norm):
    x = x_ref[...]
    h = _rmsnorm(x, g_ref[...]).astype(BF16)
    acc = x
    for c in range(wu_ref.shape[1] // FF_CHUNK):
        cols = slice(c * FF_CHUNK, (c + 1) * FF_CHUNK)
        u = jnp.maximum(_dot(h, wu_ref[:, cols]), 0.0)
        acc = acc + _dot((u * u).astype(BF16), wd_ref[cols, :])
    o_ref[...] = _rmsnorm(acc, gf_ref[...]) if final_norm else acc


def _mlp(x2, g, w_up, w_down, g_final, final_norm):
    n, d = x2.shape
    return pl.pallas_call(
        functools.partial(_mlp_kernel, final_norm=final_norm),
        grid=(n // ROW_TILE,),
        in_specs=[pl.BlockSpec((ROW_TILE, d), lambda i: (i, 0)), _resident(g.shape),
                  _resident(w_up.shape), _resident(w_down.shape), _resident(g_final.shape)],
        out_specs=pl.BlockSpec((ROW_TILE, d), lambda i: (i, 0)),
        out_shape=jax.ShapeDtypeStruct((n, d), F32),
        compiler_params=_cparams("parallel"),
        name="mlp",
    )(x2, g, w_up, w_down, g_final)


def _block_diag(w):
    eye = jnp.eye(CMP_GROUPS, dtype=w.dtype)
    out = jnp.einsum("gh,...ij->...gihj", eye, w)
    return out.reshape(*w.shape[:-2], CMP_GROUPS * w.shape[-2], CMP_GROUPS * w.shape[-1])


def _cmp_to_slc_map(nchunk, nb):
    cs = jnp.arange(nchunk) * CMP_STRIDE
    ss = jnp.arange(nb) * SLC_LEN
    ov = jnp.minimum(cs[None, :] + CMP_LEN, ss[:, None] + SLC_LEN) - jnp.maximum(cs[None, :], ss[:, None])
    return (jnp.maximum(ov, 0) / CMP_STRIDE).astype(BF16)


def _alibi_slopes():
    h = jnp.arange(1, N_HEADS + 1, dtype=F32)
    return jnp.exp2(-ALIBI_MAX * h / N_HEADS)


def _rg_layer(x, g, w_in, conv_w, conv_b, w_a, b_a, w_x, b_x, lam, w_out):
    bsz, s, d = x.shape
    row = lambda v: v.reshape(1, -1)
    y, xp = _rg_in(x.reshape(bsz * s, d), row(g), w_in.astype(BF16))
    return _rg_scan(xp.reshape(bsz, s, d), y.reshape(bsz, s, d), x, conv_w, row(conv_b),
                    w_a.astype(BF16), row(b_a), w_x.astype(BF16), row(b_x), row(lam),
                    w_out.astype(BF16))


def _nsa_layer(x, g, w_in, b_gate, pe_k, pe_v, w1_k, w2_k, w1_v, w2_v, w_out):
    bsz, s, d = x.shape
    q_cols = N_HEADS * HEAD_DIM
    kv_cols = N_KV_GROUPS * HEAD_DIM
    wq = w_in[:, :q_cols].astype(BF16)
    wc = w_in[:, q_cols:q_cols + 2 * kv_cols].astype(BF16)
    wa = w_in[:, q_cols + 2 * kv_cols:q_cols + 6 * kv_cols].astype(BF16)
    per_group = 3 * HEADS_PER_GROUP
    wg = w_in[:, q_cols + 6 * kv_cols:].reshape(d, N_KV_GROUPS, per_group)
    wg = jnp.pad(wg, ((0, 0), (0, 0), (0, 128 - per_group))).reshape(d, N_KV_GROUPS * 128)
    bg = jnp.pad(b_gate.reshape(N_KV_GROUPS, per_group), ((0, 0), (0, 128 - per_group)))
    q, kvc, kva, gates = _nsa_in(x, g.reshape(1, d), wq, wc, wa, wg.astype(BF16),
                                 bg.reshape(1, -1))

    pe = jnp.stack([jnp.tile(pe_k, (1, CMP_GROUPS)), jnp.tile(pe_v, (1, CMP_GROUPS))])
    w1 = jnp.stack([w1_k, w1_v]).reshape(2, CMP_LEN, HEAD_DIM, HEAD_DIM)
    kvcmp = _compress(kvc, pe, _block_diag(w1).astype(BF16),
                      _block_diag(jnp.stack([w2_k, w2_v])).astype(BF16))

    cmap = _cmp_to_slc_map(s // CMP_STRIDE, s // SLC_LEN)
    o = _attend(_alibi_slopes(), q, gates, kvcmp, kva, cmap)
    return _out_proj(o.reshape(bsz * s, -1), w_out.astype(BF16), x.reshape(bsz * s, d))


def kernel(x, norm_mix, norm_ffn, norm_final, rg_w_in, rg_conv_w, rg_conv_b, rg_w_a, rg_b_a, rg_w_x, rg_b_x, rg_lambda, rg_w_out, nsa_w_in, nsa_b_gate, nsa_pe_k, nsa_pe_v, nsa_w1_k, nsa_w2_k, nsa_w1_v, nsa_w2_v, nsa_w_out, mlp_w_up, mlp_w_down):
    bsz, s, d = x.shape
    depth = norm_mix.shape[0]
    for i in range(depth):
        j = i // 2
        if i % 2 == 0:
            x = _rg_layer(x, norm_mix[i], rg_w_in[j], rg_conv_w[j], rg_conv_b[j], rg_w_a[j],
                          rg_b_a[j], rg_w_x[j], rg_b_x[j], rg_lambda[j], rg_w_out[j])
        else:
            x = _nsa_layer(x, norm_mix[i], nsa_w_in[j], nsa_b_gate[j], nsa_pe_k[j], nsa_pe_v[j],
                           nsa_w1_k[j], nsa_w2_k[j], nsa_w1_v[j], nsa_w2_v[j], nsa_w_out[j])
            x = x.reshape(bsz, s, d)
        x = _mlp(x.reshape(bsz * s, d), norm_ffn[i].reshape(1, d), mlp_w_up[i].astype(BF16),
                 mlp_w_down[i].astype(BF16), norm_final.reshape(1, d),
                 final_norm=(i == depth - 1)).reshape(bsz, s, d)
    return x
```

```python
import functools
import math

import jax
import jax.numpy as jnp
from jax import lax
from jax.experimental import pallas as pl
from jax.experimental.pallas import tpu as pltpu

F32 = jnp.float32
BF16 = jnp.bfloat16

RMS_EPS = 1e-6
RG_BLOCKS = 4
CONV_W = 4
RG_C = 8.0
N_HEADS = 16
HEAD_DIM = 64
N_KV_GROUPS = 4
HEADS_PER_GROUP = N_HEADS // N_KV_GROUPS
CMP_LEN = 32
CMP_STRIDE = 16
SLC_LEN = 64
N_SELECT = 16
WINDOW = 512
ALIBI_MAX = 8.0
NEG_INF = -1e30
FORCE_BONUS = 1e4

VMEM_LIMIT_BYTES = 56 * 1024 * 1024
LANES = 128
CMP_GROUPS = LANES // HEAD_DIM

ROW_TILE = 512
SCAN_TILE = 256
Q_TILE = 128
KEY_TILE = 128
FF_CHUNK = 1024

MAX_BLOCKS = 128
AUX_COLS = 64
GATE_ROWS = 16
MASK_BIAS = 2.0 ** 100
SOFTMAX_FLOOR = -1e25
LOG2E = math.log2(math.e)
FLAG_BITS = 16
N_FORCED = 3
SEL_CHUNK = 4
FAR_POSITION = 2 ** 30


def _cparams(*sem):
    return pltpu.CompilerParams(dimension_semantics=sem, vmem_limit_bytes=VMEM_LIMIT_BYTES)


def _resident(shape):
    nd = len(shape)
    return pl.BlockSpec(shape, lambda *_: (0,) * nd, pipeline_mode=pl.Buffered(1))


def _rmsnorm(x, g):
    y = x * lax.rsqrt(jnp.mean(x * x, axis=-1, keepdims=True) + RMS_EPS)
    return y * g


def _gelu_tanh(x):
    return jax.nn.gelu(x, approximate=True)


def _dot(a, b):
    return jnp.dot(a, b, preferred_element_type=F32)


def _dot_nt(a, b):
    return lax.dot_general(a, b, (((1,), (1,)), ((), ())), preferred_element_type=F32)


def _rg_in_kernel(x_ref, g_ref, w_ref, y_ref, xp_ref):
    d = y_ref.shape[-1]
    h = _rmsnorm(x_ref[...], g_ref[...]).astype(BF16)
    proj = _dot(h, w_ref[...])
    y_ref[...] = _gelu_tanh(proj[:, :d])
    xp_ref[...] = proj[:, d:]


def _rg_in(x2, g, w_in):
    n, d = x2.shape
    return pl.pallas_call(
        _rg_in_kernel,
        grid=(n // ROW_TILE,),
        in_specs=[
            pl.BlockSpec((ROW_TILE, d), lambda i: (i, 0)),
            _resident((1, d)),
            _resident(w_in.shape),
        ],
        out_specs=[pl.BlockSpec((ROW_TILE, d), lambda i: (i, 0))] * 2,
        out_shape=[jax.ShapeDtypeStruct((n, d), F32)] * 2,
        compiler_params=_cparams("parallel"),
        name="rg_in",
    )(x2, g, w_in)


def _shift_rows(x, k, fill):
    rows = lax.broadcasted_iota(jnp.int32, x.shape, 0)
    return jnp.where(rows >= k, pltpu.roll(x, k, axis=0), fill)


def _rg_scan_kernel(xp_ref, y_ref, x_ref, cw_ref, cb_ref, wa_ref, ba_ref, wx_ref, bx_ref,
                    lam_ref, wo_ref, o_ref, tail_ref, h_ref):
    ts, d = xp_ref.shape
    bw = d // RG_BLOCKS

    @pl.when(pl.program_id(1) == 0)
    def _():
        tail_ref[...] = jnp.zeros_like(tail_ref)
        h_ref[...] = jnp.zeros_like(h_ref)

    xp = xp_ref[...]
    ext = jnp.concatenate([tail_ref[...], xp], axis=0)
    cw = cw_ref[...]
    xb = cb_ref[...] + cw[CONV_W - 1:CONV_W] * xp
    for k in range(1, CONV_W):
        xb = xb + cw[CONV_W - 1 - k:CONV_W - k] * pltpu.roll(ext, k, axis=0)[8:]
    tail_ref[...] = xp[ts - 8:]

    xb16 = xb.astype(BF16)
    ra = jnp.concatenate(
        [_dot(xb16[:, n * bw:(n + 1) * bw], wa_ref[n]) for n in range(RG_BLOCKS)], axis=1)
    rx = jnp.concatenate(
        [_dot(xb16[:, n * bw:(n + 1) * bw], wx_ref[n]) for n in range(RG_BLOCKS)], axis=1)
    r = jax.nn.sigmoid(ra + ba_ref[...])
    i = jax.nn.sigmoid(rx + bx_ref[...])
    nl = -lam_ref[...]
    softplus = jnp.maximum(nl, 0.0) + jnp.log1p(jnp.exp(-jnp.abs(nl)))
    log_a = -RG_C * r * softplus
    a = jnp.exp(log_a)
    b = jnp.sqrt(-jnp.tanh(log_a) * (a * a + 1.0)) * (i * xb)

    k = 1
    while k < ts:
        b = b + a * _shift_rows(b, k, 0.0)
        a = a * _shift_rows(a, k, 1.0)
        k *= 2
    hs = a * h_ref[0:1, :] + b
    h_ref[...] = jnp.broadcast_to(hs[ts - 1:ts, :], h_ref.shape)

    o_ref[...] = x_ref[...] + _dot((hs * y_ref[...]).astype(BF16), wo_ref[...])


def _rg_scan(xp, y, x, conv_w, conv_b, w_a, b_a, w_x, b_x, lam, w_out):
    bsz, s, d = x.shape
    tile = lambda: pl.BlockSpec((None, SCAN_TILE, d), lambda b, i: (b, i, 0))
    return pl.pallas_call(
        _rg_scan_kernel,
        grid=(bsz, s // SCAN_TILE),
        in_specs=[tile(), tile(), tile(),
                  _resident(conv_w.shape), _resident(conv_b.shape),
                  _resident(w_a.shape), _resident(b_a.shape),
                  _resident(w_x.shape), _resident(b_x.shape),
                  _resident(lam.shape), _resident(w_out.shape)],
        out_specs=tile(),
        out_shape=jax.ShapeDtypeStruct((bsz, s, d), F32),
        scratch_shapes=[pltpu.VMEM((8, d), F32), pltpu.VMEM((8, d), F32)],
        compiler_params=_cparams("parallel", "arbitrary"),
        name="rg_scan",
    )(xp, y, x, conv_w, conv_b, w_a, b_a, w_x, b_x, lam, w_out)


def _pos_terms(pos):
    lane = lax.broadcasted_iota(jnp.int32, (pos.shape[0], AUX_COLS), 1)
    hi = ((pos // SLC_LEN) * SLC_LEN).astype(F32)
    lo = (pos % SLC_LEN).astype(F32)
    return jnp.where(lane < 4, jnp.where(lane % 2 == 0, hi, lo), 0.0).astype(BF16)


def _nsa_in_kernel(x_ref, g_ref, wq_ref, wc_ref, wk_ref, wv_ref, wg_ref, bg_ref,
                   qt_ref, kvc_ref, ks_ref, kw_ref, vst_ref, vwt_ref, gt_ref):
    tm = x_ref.shape[0]
    dh = HEAD_DIM
    ng = ks_ref.shape[0]
    h = _rmsnorm(x_ref[...], g_ref[...]).astype(BF16)
    qt_ref[...] = (_dot_nt(wq_ref[...], h) * (dh ** -0.5 * LOG2E)).astype(qt_ref.dtype)
    kvc_ref[...] = _dot(h, wc_ref[...])
    gt_ref[...] = jax.nn.sigmoid(_dot_nt(wg_ref[...], h) + bg_ref[...])

    pos = pl.program_id(1) * tm + lax.broadcasted_iota(jnp.int32, (tm, 1), 0)
    terms = _pos_terms(pos)
    blk_lane = lax.broadcasted_iota(jnp.int32, (tm, MAX_BLOCKS), 1)
    onehot = jnp.where(blk_lane == pos // SLC_LEN, 1.0, 0.0).astype(BF16)
    k = _dot(h, wk_ref[...]).astype(BF16)
    for gi in range(ng):
        ks_pos = jnp.concatenate([k[:, gi * dh:(gi + 1) * dh], terms], axis=1)
        ks_ref[gi] = jnp.concatenate([onehot, ks_pos], axis=1)
        kw_ref[gi] = jnp.concatenate([k[:, (ng + gi) * dh:(ng + gi + 1) * dh], terms], axis=1)
    vt = _dot_nt(wv_ref[...], h).astype(BF16)
    for gi in range(ng):
        for c in range(tm // KEY_TILE):
            cols = slice(c * KEY_TILE, (c + 1) * KEY_TILE)
            vst_ref[gi, c] = vt[gi * dh:(gi + 1) * dh, cols]
            vwt_ref[gi, c] = vt[(ng + gi) * dh:(ng + gi + 1) * dh, cols]


def _nsa_in(x, g, wq_t, wc, wk, wv_t, wg_t, bg_t):
    bsz, s, d = x.shape
    ng, dh, kt = N_KV_GROUPS, HEAD_DIM, KEY_TILE
    per = ROW_TILE // kt
    return pl.pallas_call(
        _nsa_in_kernel,
        grid=(bsz, s // ROW_TILE),
        in_specs=[pl.BlockSpec((None, ROW_TILE, d), lambda b, i: (b, i, 0)),
                  _resident(g.shape), _resident(wq_t.shape), _resident(wc.shape),
                  _resident(wk.shape), _resident(wv_t.shape), _resident(wg_t.shape),
                  _resident(bg_t.shape)],
        out_specs=[
            pl.BlockSpec((None, wq_t.shape[0], ROW_TILE), lambda b, i: (b, 0, i)),
            pl.BlockSpec((None, ROW_TILE, wc.shape[1]), lambda b, i: (b, i, 0)),
            pl.BlockSpec((None, ng, ROW_TILE, dh + MAX_BLOCKS + AUX_COLS), lambda b, i: (b, 0, i, 0)),
            pl.BlockSpec((None, ng, ROW_TILE, dh + AUX_COLS), lambda b, i: (b, 0, i, 0)),
            pl.BlockSpec((None, ng, per, dh, kt), lambda b, i: (b, 0, i, 0, 0)),
            pl.BlockSpec((None, ng, per, dh, kt), lambda b, i: (b, 0, i, 0, 0)),
            pl.BlockSpec((None, wg_t.shape[0], ROW_TILE), lambda b, i: (b, 0, i)),
        ],
        out_shape=[
            jax.ShapeDtypeStruct((bsz, wq_t.shape[0], s), BF16),
            jax.ShapeDtypeStruct((bsz, s, wc.shape[1]), F32),
            jax.ShapeDtypeStruct((bsz, ng, s, dh + MAX_BLOCKS + AUX_COLS), BF16),
            jax.ShapeDtypeStruct((bsz, ng, s, dh + AUX_COLS), BF16),
            jax.ShapeDtypeStruct((bsz, ng, s // kt, dh, kt), BF16),
            jax.ShapeDtypeStruct((bsz, ng, s // kt, dh, kt), BF16),
            jax.ShapeDtypeStruct((bsz, wg_t.shape[0], s), F32),
        ],
        compiler_params=_cparams("parallel", "parallel"),
        name="nsa_in",
    )(x, g, wq_t, wc, wk, wv_t, wg_t, bg_t)


def _compress_kernel(zk_ref, zv_ref, pe_ref, w1_ref, w2_ref, kc_ref, vct_ref):
    nchunk = kc_ref.shape[1]
    dh = HEAD_DIM

    def pre_act(z_ref, t):
        acc = [jnp.zeros((nchunk, LANES), F32) for _ in range(CMP_LEN // CMP_STRIDE)]
        for p in range(CMP_STRIDE):
            zp = z_ref[pl.ds(p, nchunk, stride=CMP_STRIDE), :]
            for r in range(len(acc)):
                pp = r * CMP_STRIDE + p
                acc[r] = acc[r] + _dot((zp + pe_ref[t, pp:pp + 1, :]).astype(BF16), w1_ref[t, pp])
        return _gelu_tanh(acc[0] + pltpu.roll(acc[1], nchunk - 1, axis=0)).astype(BF16)

    kc = _dot(pre_act(zk_ref, 0), w2_ref[0]).astype(BF16)
    cmp_end = lax.broadcasted_iota(jnp.int32, (nchunk, 1), 0) * CMP_STRIDE + (CMP_LEN - 1)
    terms = _pos_terms(cmp_end)
    for gi in range(kc_ref.shape[0]):
        kc_ref[gi] = jnp.concatenate([kc[:, gi * dh:(gi + 1) * dh], terms], axis=1)
    vct = _dot_nt(w2_ref[1], pre_act(zv_ref, 1)).astype(BF16)
    for gi in range(vct_ref.shape[0]):
        vct_ref[gi] = vct[gi * dh:(gi + 1) * dh, :]


def _compress(kvc, pe, w1, w2):
    bsz, s, _ = kvc.shape
    nchunk = s // CMP_STRIDE
    halves = N_KV_GROUPS // CMP_GROUPS
    return pl.pallas_call(
        _compress_kernel,
        grid=(bsz, halves),
        in_specs=[
            pl.BlockSpec((None, s, LANES), lambda b, c: (b, 0, c)),
            pl.BlockSpec((None, s, LANES), lambda b, c: (b, 0, halves + c)),
            _resident(pe.shape), _resident(w1.shape), _resident(w2.shape),
        ],
        out_specs=[
            pl.BlockSpec((None, CMP_GROUPS, nchunk, HEAD_DIM + AUX_COLS), lambda b, c: (b, c, 0, 0)),
            pl.BlockSpec((None, CMP_GROUPS, HEAD_DIM, nchunk), lambda b, c: (b, c, 0, 0)),
        ],
        out_shape=[
            jax.ShapeDtypeStruct((bsz, N_KV_GROUPS, nchunk, HEAD_DIM + AUX_COLS), BF16),
            jax.ShapeDtypeStruct((bsz, N_KV_GROUPS, HEAD_DIM, nchunk), BF16),
        ],
        compiler_params=_cparams("parallel", "parallel"),
        name="nsa_compress",
    )(kvc, kvc, pe, w1, w2)


def _split3(x):
    hi = x.astype(BF16)
    r = x - hi.astype(F32)
    mid = r.astype(BF16)
    lo = (r - mid.astype(F32)).astype(BF16)
    return hi, mid, lo


def _tile_scores(k_ref, q_aug, tiles, t_col, window):
    tk = KEY_TILE
    key_row = lax.broadcasted_iota(jnp.int32, (tk, 1), 0)
    out = []
    for p in tiles:
        k0 = pl.multiple_of(jnp.maximum(p, 0) * tk, tk)
        pos = jnp.where(p >= 0, k0, FAR_POSITION) + key_row
        ok = pos <= t_col
        if window:
            ok = ok & (t_col - pos < WINDOW)
        out.append(jnp.where(ok, _dot(k_ref[pl.ds(k0, tk), :], q_aug), NEG_INF))
    return out


def _tile_values(vt_ref, tiles, probs):
    acc = None
    for p, pr in zip(tiles, probs):
        pv = _dot(vt_ref[jnp.maximum(p, 0)], pr.astype(BF16))
        acc = pv if acc is None else acc + pv
    return acc


def _attn_kernel(qt_ref, gt_ref, kc_ref, vct_ref, ks_ref, vst_ref, kw_ref, vwt_ref, cmap_ref,
                 slope_ref, o_ref, list_ref, ms_ref, ls_ref, as_ref, *, n_sel):
    hg, dh, tq, tk = HEADS_PER_GROUP, HEAD_DIM, Q_TILE, KEY_TILE
    ncol = hg * tq
    nkt = vst_ref.shape[0]
    qb = pl.program_id(2)
    t0 = qb * tq
    t_col = t0 + lax.broadcasted_iota(jnp.int32, (1, ncol), 1) % tq

    qt = qt_ref[...]
    q_rows = jnp.concatenate([qt[h * dh:(h + 1) * dh, :] for h in range(hg)], axis=1)
    slope_rows = slope_ref[...]
    q_pos = jnp.concatenate([q_rows, slope_rows], axis=0)

    nc = kc_ref.shape[0]
    s_c = _dot(kc_ref[...], q_pos)
    cmp_end = lax.broadcasted_iota(jnp.int32, (nc, 1), 0) * CMP_STRIDE + (CMP_LEN - 1)
    m_c = cmp_end <= t_col
    s_c = jnp.where(m_c, s_c, NEG_INF)
    p_c = jnp.where(m_c, jnp.exp2(s_c - jnp.max(s_c, axis=0, keepdims=True)), 0.0)
    p_c = p_c * (1.0 / jnp.maximum(jnp.sum(p_c, axis=0, keepdims=True), 1e-30))
    o_c = _dot(vct_ref[...], p_c.astype(BF16))

    nb = cmap_ref.shape[0]
    p_sum = p_c[:, 0:tq]
    for h in range(1, hg):
        p_sum = p_sum + p_c[:, h * tq:(h + 1) * tq]
    cmap = cmap_ref[...]
    imp = sum(_dot(cmap, part) for part in _split3(p_sum))
    blk = lax.broadcasted_iota(jnp.int32, (nb, tq), 0)
    cur = (t0 + lax.broadcasted_iota(jnp.int32, (1, tq), 1)) // SLC_LEN
    forced = (blk == 0) | (blk == cur) | (blk == cur - 1)
    visible = blk <= cur
    sel = jnp.where(forced & visible, 1.0, 0.0)
    score = jnp.where(visible, jnp.where(forced, -jnp.inf, imp), -1.0)
    blk_f = blk.astype(F32)
    for _ in range(n_sel - N_FORCED):
        best = jnp.max(score, axis=0, keepdims=True)
        first = jnp.min(jnp.where(score == best, blk_f, float(nb)), axis=0, keepdims=True)
        pick = blk_f == first
        sel = jnp.where(pick, 1.0, sel)
        score = jnp.where(pick, -jnp.inf, score)
    sel = jnp.where(visible, sel, 0.0)

    cnt = _dot_nt(jnp.ones((8, tq), BF16), sel.astype(BF16))
    blk_any = jnp.where(cnt > 0.0, 1.0, 0.0).astype(BF16)
    r_i = lax.broadcasted_iota(jnp.int32, (nb, LANES), 0)
    c_i = lax.broadcasted_iota(jnp.int32, (nb, LANES), 1)
    bit = jnp.left_shift(1, r_i % FLAG_BITS).astype(F32)
    words = _dot(blk_any, jnp.where(r_i // FLAG_BITS == c_i, bit, 0.0).astype(BF16))
    words = [words[0, w].astype(jnp.int32) for w in range(nb // FLAG_BITS)]

    per_tile = tk // SLC_LEN
    n_before = t0 // tk
    n = jnp.int32(0)
    for p in range(nkt - tq // tk):
        word = words[p * per_tile // FLAG_BITS]
        hit = (jnp.right_shift(word, p * per_tile % FLAG_BITS) & (2 ** per_tile - 1)) != 0
        list_ref[n] = jnp.int32(p)
        n = n + (hit & (p < n_before)).astype(jnp.int32)
    for j in range(tq // tk):
        list_ref[n + j] = n_before + j
    n = n + tq // tk
    for j in range(SEL_CHUNK - 1):
        list_ref[n + j] = jnp.int32(-1)

    w_tiles = [n_before - WINDOW // tk + j for j in range((WINDOW + tq) // tk)]
    s_w = _tile_scores(kw_ref, q_pos, w_tiles, t_col, window=True)
    m_w = functools.reduce(jnp.maximum, [jnp.max(s, axis=0, keepdims=True) for s in s_w])
    p_w = [jnp.exp2(s - m_w) for s in s_w]
    l_w = sum(jnp.sum(p, axis=0, keepdims=True) for p in p_w)
    o_w = _tile_values(vwt_ref, w_tiles, p_w) * (1.0 / jnp.maximum(l_w, 1e-30))

    bias = jnp.where(sel > 0.0, 0.0, -MASK_BIAS).astype(BF16)
    q_sel = jnp.concatenate([jnp.concatenate([bias] * hg, axis=1), q_pos], axis=0)
    ms_ref[...] = jnp.full(ms_ref.shape, SOFTMAX_FLOOR, F32)
    ls_ref[...] = jnp.zeros(ls_ref.shape, F32)
    as_ref[...] = jnp.zeros(as_ref.shape, F32)

    def sel_chunk(c, carry):
        tiles = [list_ref[c * SEL_CHUNK + j] for j in range(SEL_CHUNK)]
        s_s = _tile_scores(ks_ref, q_sel, tiles, t_col, window=False)
        m_prev = ms_ref[...]
        m_new = functools.reduce(jnp.maximum, [jnp.max(s, axis=0, keepdims=True) for s in s_s],
                                 m_prev)
        p_s = [jnp.exp2(s - m_new) for s in s_s]
        alpha = jnp.exp2(m_prev - m_new)
        ms_ref[...] = m_new
        ls_ref[...] = alpha * ls_ref[...] + sum(jnp.sum(p, axis=0, keepdims=True) for p in p_s)
        as_ref[...] = alpha * as_ref[...] + _tile_values(vst_ref, tiles, p_s)
        return carry

    lax.fori_loop(0, (n + SEL_CHUNK - 1) // SEL_CHUNK, sel_chunk, 0)
    o_s = as_ref[...] * (1.0 / jnp.maximum(ls_ref[...], 1e-30))

    gt = gt_ref[...]
    out = []
    for h in range(hg):
        cols = slice(h * tq, (h + 1) * tq)
        r = 3 * h
        o_h = gt[r:r + 1] * o_c[:, cols] + gt[r + 1:r + 2] * o_s[:, cols] + gt[r + 2:r + 3] * o_w[:, cols]
        out.append(o_h.T)
    o_ref[...] = jnp.concatenate(out, axis=1).astype(o_ref.dtype)


def _attend(qt, gates_t, kc, vct, ks, vst, kw, vwt, cmap, slopes):
    bsz, _, s = qt.shape
    hg, dh, tq = HEADS_PER_GROUP, HEAD_DIM, Q_TILE
    nchunk = kc.shape[2]
    n_sel = min(N_SELECT, s // SLC_LEN)
    nkt = s // KEY_TILE
    per_bg = lambda *tail: pl.BlockSpec((None, None) + tail, lambda b, g, i: (b, g) + (0,) * len(tail))
    col = lambda: pltpu.VMEM((1, hg * tq), F32)
    acc = lambda: pltpu.VMEM((dh, hg * tq), F32)
    return pl.pallas_call(
        functools.partial(_attn_kernel, n_sel=n_sel),
        grid=(bsz, N_KV_GROUPS, s // tq),
        in_specs=[
            pl.BlockSpec((None, hg * dh, tq), lambda b, g, i: (b, g, i)),
            pl.BlockSpec((None, GATE_ROWS, tq), lambda b, g, i: (b, g, i)),
            per_bg(nchunk, dh + AUX_COLS), per_bg(dh, nchunk),
            per_bg(s, dh + MAX_BLOCKS + AUX_COLS), per_bg(nkt, dh, KEY_TILE),
            per_bg(s, dh + AUX_COLS), per_bg(nkt, dh, KEY_TILE),
            pl.BlockSpec(cmap.shape, lambda b, g, i: (0, 0)),
            pl.BlockSpec((None, AUX_COLS, hg * tq), lambda b, g, i: (g, 0, 0)),
        ],
        out_specs=pl.BlockSpec((None, tq, hg * dh), lambda b, g, i: (b, i, g)),
        out_shape=jax.ShapeDtypeStruct((bsz, s, N_HEADS * dh), BF16),
        scratch_shapes=[pltpu.SMEM((nkt + SEL_CHUNK,), jnp.int32), col(), col(), acc()],
        compiler_params=_cparams("parallel", "parallel", "arbitrary"),
        name="nsa_attend",
    )(qt, gates_t, kc, vct, ks, vst, kw, vwt, cmap, slopes)


def _out_proj_kernel(o_ref, w_ref, x_ref, y_ref):
    y_ref[...] = x_ref[...] + _dot(o_ref[...], w_ref[...])


def _out_proj(o2, w, x2):
    n, d = x2.shape
    return pl.pallas_call(
        _out_proj_kernel,
        grid=(n // ROW_TILE,),
        in_specs=[pl.BlockSpec((ROW_TILE, o2.shape[1]), lambda i: (i, 0)), _resident(w.shape),
                  pl.BlockSpec((ROW_TILE, d), lambda i: (i, 0))],
        out_specs=pl.BlockSpec((ROW_TILE, d), lambda i: (i, 0)),
        out_shape=jax.ShapeDtypeStruct((n, d), F32),
        compiler_params=_cparams("parallel"),
        name="nsa_out",
    )(o2, w, x2)


def _mlp_kernel(x_ref, g_ref, wu_ref, wd_ref, gf_ref, o_ref, *, final_norm):
    x = x_ref[...]
    h = _rmsnorm(x, g_ref[...]).astype(BF16)
    acc = x
    for c in range(wu_ref.shape[1] // FF_CHUNK):
        cols = slice(c * FF_CHUNK, (c + 1) * FF_CHUNK)
        u = jnp.maximum(_dot(h, wu_ref[:, cols]), 0.0)
        acc = acc + _dot((u * u).astype(BF16), wd_ref[cols, :])
    o_ref[...] = _rmsnorm(acc, gf_ref[...]) if final_norm else acc


def _mlp(x2, g, w_up, w_down, g_final, final_norm):
    n, d = x2.shape
    return pl.pallas_call(
        functools.partial(_mlp_kernel, final_norm=final_norm),
        grid=(n // ROW_TILE,),
        in_specs=[pl.BlockSpec((ROW_TILE, d), lambda i: (i, 0)), _resident(g.shape),
                  _resident(w_up.shape), _resident(w_down.shape), _resident(g_final.shape)],
        out_specs=pl.BlockSpec((ROW_TILE, d), lambda i: (i, 0)),
        out_shape=jax.ShapeDtypeStruct((n, d), F32),
        compiler_params=_cparams("parallel"),
        name="mlp",
    )(x2, g, w_up, w_down, g_final)


def _block_diag(w):
    eye = jnp.eye(CMP_GROUPS, dtype=w.dtype)
    out = jnp.einsum("gh,...ij->...gihj", eye, w)
    return out.reshape(*w.shape[:-2], CMP_GROUPS * w.shape[-2], CMP_GROUPS * w.shape[-1])


def _cmp_to_slc_map(nchunk, nb):
    cs = jnp.arange(nchunk) * CMP_STRIDE
    ss = jnp.arange(MAX_BLOCKS) * SLC_LEN
    ov = jnp.minimum(cs[None, :] + CMP_LEN, ss[:, None] + SLC_LEN) - jnp.maximum(cs[None, :], ss[:, None])
    ov = jnp.where(jnp.arange(MAX_BLOCKS)[:, None] < nb, ov, 0)
    return (jnp.maximum(ov, 0) / CMP_STRIDE).astype(BF16)


def _alibi_slope_rows():
    h = jnp.arange(1, N_HEADS + 1, dtype=F32)
    slope = jnp.exp2(-ALIBI_MAX * h / N_HEADS) * LOG2E
    hi = slope.astype(BF16)
    lo = (slope - hi.astype(F32)).astype(BF16)
    rows = jnp.stack([hi, hi, lo, lo], axis=0)
    rows = jnp.pad(rows, ((0, AUX_COLS - 4), (0, 0)))
    rows = rows.reshape(AUX_COLS, N_KV_GROUPS, HEADS_PER_GROUP).transpose(1, 0, 2)
    return jnp.repeat(rows, Q_TILE, axis=2)


def _rg_layer(x, g, w_in, conv_w, conv_b, w_a, b_a, w_x, b_x, lam, w_out):
    bsz, s, d = x.shape
    row = lambda v: v.reshape(1, -1)
    y, xp = _rg_in(x.reshape(bsz * s, d), row(g), w_in.astype(BF16))
    return _rg_scan(xp.reshape(bsz, s, d), y.reshape(bsz, s, d), x, conv_w, row(conv_b),
                    w_a.astype(BF16), row(b_a), w_x.astype(BF16), row(b_x), row(lam),
                    w_out.astype(BF16))


def _nsa_layer(x, g, w_in, b_gate, pe_k, pe_v, w1_k, w2_k, w1_v, w2_v, w_out):
    bsz, s, d = x.shape
    assert s // SLC_LEN <= MAX_BLOCKS and s % ROW_TILE == 0
    q_cols = N_HEADS * HEAD_DIM
    kv_cols = N_KV_GROUPS * HEAD_DIM
    kv = w_in[:, q_cols:q_cols + 6 * kv_cols].reshape(d, 6, kv_cols)
    wq_t = w_in[:, :q_cols].T.astype(BF16)
    wc = kv[:, 0:2].reshape(d, 2 * kv_cols).astype(BF16)
    wk = jnp.concatenate([kv[:, 2], kv[:, 4]], axis=1).astype(BF16)
    wv_t = jnp.concatenate([kv[:, 3], kv[:, 5]], axis=1).T.astype(BF16)
    per_group = 3 * HEADS_PER_GROUP
    wg = w_in[:, q_cols + 6 * kv_cols:].reshape(d, N_KV_GROUPS, per_group)
    wg = jnp.pad(wg, ((0, 0), (0, 0), (0, GATE_ROWS - per_group))).reshape(d, -1)
    bg = jnp.pad(b_gate.reshape(N_KV_GROUPS, per_group), ((0, 0), (0, GATE_ROWS - per_group)))
    qt, kvc, ks, kw, vst, vwt, gates_t = _nsa_in(
        x, g.reshape(1, d), wq_t, wc, wk, wv_t, wg.T.astype(BF16), bg.reshape(-1, 1))

    pe = jnp.stack([jnp.tile(pe_k, (1, CMP_GROUPS)), jnp.tile(pe_v, (1, CMP_GROUPS))])
    w1 = jnp.stack([w1_k, w1_v]).reshape(2, CMP_LEN, HEAD_DIM, HEAD_DIM)
    w2 = jnp.stack([_block_diag(w2_k), _block_diag(w2_v).T])
    kc, vct = _compress(kvc, pe, _block_diag(w1).astype(BF16), w2.astype(BF16))

    cmap = _cmp_to_slc_map(s // CMP_STRIDE, s // SLC_LEN)
    o = _attend(qt, gates_t, kc, vct, ks, vst, kw, vwt, cmap, _alibi_slope_rows())
    return _out_proj(o.reshape(bsz * s, -1), w_out.astype(BF16), x.reshape(bsz * s, d))


def kernel(x, norm_mix, norm_ffn, norm_final, rg_w_in, rg_conv_w, rg_conv_b, rg_w_a, rg_b_a, rg_w_x, rg_b_x, rg_lambda, rg_w_out, nsa_w_in, nsa_b_gate, nsa_pe_k, nsa_pe_v, nsa_w1_k, nsa_w2_k, nsa_w1_v, nsa_w2_v, nsa_w_out, mlp_w_up, mlp_w_down):
    bsz, s, d = x.shape
    depth = norm_mix.shape[0]
    for i in range(depth):
        j = i // 2
        if i % 2 == 0:
            x = _rg_layer(x, norm_mix[i], rg_w_in[j], rg_conv_w[j], rg_conv_b[j], rg_w_a[j],
                          rg_b_a[j], rg_w_x[j], rg_b_x[j], rg_lambda[j], rg_w_out[j])
        else:
            x = _nsa_layer(x, norm_mix[i], nsa_w_in[j], nsa_b_gate[j], nsa_pe_k[j], nsa_pe_v[j],
                           nsa_w1_k[j], nsa_w2_k[j], nsa_w1_v[j], nsa_w2_v[j], nsa_w_out[j])
            x = x.reshape(bsz, s, d)
        x = _mlp(x.reshape(bsz * s, d), norm_ffn[i].reshape(1, d), mlp_w_up[i].astype(BF16),
                 mlp_w_down[i].astype(BF16), norm_final.reshape(1, d),
                 final_norm=(i == depth - 1)).reshape(bsz, s, d)
    return x
```

```python
import functools
import math

import jax
import jax.numpy as jnp
from jax import lax
from jax.experimental import pallas as pl
from jax.experimental.pallas import tpu as pltpu

F32 = jnp.float32
BF16 = jnp.bfloat16

RMS_EPS = 1e-6
RG_BLOCKS = 4
CONV_W = 4
RG_C = 8.0
N_HEADS = 16
HEAD_DIM = 64
N_KV_GROUPS = 4
HEADS_PER_GROUP = N_HEADS // N_KV_GROUPS
CMP_LEN = 32
CMP_STRIDE = 16
SLC_LEN = 64
N_SELECT = 16
WINDOW = 512
ALIBI_MAX = 8.0
NEG_INF = -1e30
FORCE_BONUS = 1e4

VMEM_LIMIT_BYTES = 56 * 1024 * 1024
LANES = 128
CMP_GROUPS = LANES // HEAD_DIM

ROW_TILE = 512
SCAN_TILE = 256
Q_TILE = 256
KEY_TILE = 128
FF_CHUNK = 1024

MAX_BLOCKS = 128
AUX_COLS = 64
GATE_ROWS = 16
MASK_BIAS = 2.0 ** 100
NULL_COL = 4
V_PAD = 16
LOG2E = math.log2(math.e)
FLAG_BITS = 16
N_FORCED = 3
SEL_CHUNK = 4
FAR_POSITION = 2 ** 30


def _cparams(*sem):
    return pltpu.CompilerParams(dimension_semantics=sem, vmem_limit_bytes=VMEM_LIMIT_BYTES)


def _resident(shape):
    nd = len(shape)
    return pl.BlockSpec(shape, lambda *_: (0,) * nd, pipeline_mode=pl.Buffered(1))


def _rmsnorm(x, g):
    y = x * lax.rsqrt(jnp.mean(x * x, axis=-1, keepdims=True) + RMS_EPS)
    return y * g


def _gelu_tanh(x):
    return jax.nn.gelu(x, approximate=True)


def _dot(a, b):
    return jnp.dot(a, b, preferred_element_type=F32)


def _dot_nt(a, b):
    return lax.dot_general(a, b, (((1,), (1,)), ((), ())), preferred_element_type=F32)


def _rg_in_kernel(x_ref, g_ref, w_ref, y_ref, xp_ref):
    d = y_ref.shape[-1]
    h = _rmsnorm(x_ref[...], g_ref[...]).astype(BF16)
    proj = _dot(h, w_ref[...])
    y_ref[...] = _gelu_tanh(proj[:, :d])
    xp_ref[...] = proj[:, d:]


def _rg_in(x2, g, w_in):
    n, d = x2.shape
    return pl.pallas_call(
        _rg_in_kernel,
        grid=(n // ROW_TILE,),
        in_specs=[
            pl.BlockSpec((ROW_TILE, d), lambda i: (i, 0)),
            _resident((1, d)),
            _resident(w_in.shape),
        ],
        out_specs=[pl.BlockSpec((ROW_TILE, d), lambda i: (i, 0))] * 2,
        out_shape=[jax.ShapeDtypeStruct((n, d), F32)] * 2,
        compiler_params=_cparams("parallel"),
        name="rg_in",
    )(x2, g, w_in)


def _shift_rows(x, k, fill):
    rows = lax.broadcasted_iota(jnp.int32, x.shape, 0)
    return jnp.where(rows >= k, pltpu.roll(x, k, axis=0), fill)


def _rg_scan_kernel(xp_ref, y_ref, x_ref, cw_ref, cb_ref, wa_ref, ba_ref, wx_ref, bx_ref,
                    lam_ref, wo_ref, o_ref, tail_ref, h_ref):
    ts, d = xp_ref.shape
    bw = d // RG_BLOCKS

    @pl.when(pl.program_id(1) == 0)
    def _():
        tail_ref[...] = jnp.zeros_like(tail_ref)
        h_ref[...] = jnp.zeros_like(h_ref)

    xp = xp_ref[...]
    ext = jnp.concatenate([tail_ref[...], xp], axis=0)
    cw = cw_ref[...]
    xb = cb_ref[...] + cw[CONV_W - 1:CONV_W] * xp
    for k in range(1, CONV_W):
        xb = xb + cw[CONV_W - 1 - k:CONV_W - k] * pltpu.roll(ext, k, axis=0)[8:]
    tail_ref[...] = xp[ts - 8:]

    xb16 = xb.astype(BF16)
    ra = jnp.concatenate(
        [_dot(xb16[:, n * bw:(n + 1) * bw], wa_ref[n]) for n in range(RG_BLOCKS)], axis=1)
    rx = jnp.concatenate(
        [_dot(xb16[:, n * bw:(n + 1) * bw], wx_ref[n]) for n in range(RG_BLOCKS)], axis=1)
    r = jax.nn.sigmoid(ra + ba_ref[...])
    i = jax.nn.sigmoid(rx + bx_ref[...])
    nl = -lam_ref[...]
    softplus = jnp.maximum(nl, 0.0) + jnp.log1p(jnp.exp(-jnp.abs(nl)))
    log_a = -RG_C * r * softplus
    a = jnp.exp(log_a)
    b = jnp.sqrt(-jnp.tanh(log_a) * (a * a + 1.0)) * (i * xb)

    k = 1
    while k < ts:
        b = b + a * _shift_rows(b, k, 0.0)
        a = a * _shift_rows(a, k, 1.0)
        k *= 2
    hs = a * h_ref[0:1, :] + b
    h_ref[...] = jnp.broadcast_to(hs[ts - 1:ts, :], h_ref.shape)

    o_ref[...] = x_ref[...] + _dot((hs * y_ref[...]).astype(BF16), wo_ref[...])


def _rg_scan(xp, y, x, conv_w, conv_b, w_a, b_a, w_x, b_x, lam, w_out):
    bsz, s, d = x.shape
    tile = lambda: pl.BlockSpec((None, SCAN_TILE, d), lambda b, i: (b, i, 0))
    return pl.pallas_call(
        _rg_scan_kernel,
        grid=(bsz, s // SCAN_TILE),
        in_specs=[tile(), tile(), tile(),
                  _resident(conv_w.shape), _resident(conv_b.shape),
                  _resident(w_a.shape), _resident(b_a.shape),
                  _resident(w_x.shape), _resident(b_x.shape),
                  _resident(lam.shape), _resident(w_out.shape)],
        out_specs=tile(),
        out_shape=jax.ShapeDtypeStruct((bsz, s, d), F32),
        scratch_shapes=[pltpu.VMEM((8, d), F32), pltpu.VMEM((8, d), F32)],
        compiler_params=_cparams("parallel", "arbitrary"),
        name="rg_scan",
    )(xp, y, x, conv_w, conv_b, w_a, b_a, w_x, b_x, lam, w_out)


def _pos_terms(pos):
    lane = lax.broadcasted_iota(jnp.int32, (pos.shape[0], AUX_COLS), 1)
    hi = ((pos // SLC_LEN) * SLC_LEN).astype(F32)
    lo = (pos % SLC_LEN).astype(F32)
    return jnp.where(lane < 4, jnp.where(lane % 2 == 0, hi, lo), 0.0).astype(BF16)


def _nsa_in_kernel(x_ref, g_ref, wq_ref, wc_ref, wk_ref, wv_ref, wg_ref, bg_ref,
                   qt_ref, kvc_ref, ks_ref, kw_ref, vst_ref, vwt_ref, gt_ref):
    tm = x_ref.shape[0]
    dh = HEAD_DIM
    ng = ks_ref.shape[0]
    h = _rmsnorm(x_ref[...], g_ref[...]).astype(BF16)
    qt_ref[...] = (_dot_nt(wq_ref[...], h) * (dh ** -0.5 * LOG2E)).astype(qt_ref.dtype)
    kvc_ref[...] = _dot(h, wc_ref[...])
    gt_ref[...] = jax.nn.sigmoid(_dot_nt(wg_ref[...], h) + bg_ref[...])

    pos = pl.program_id(1) * tm + lax.broadcasted_iota(jnp.int32, (tm, 1), 0)
    terms = _pos_terms(pos)
    blk_lane = lax.broadcasted_iota(jnp.int32, (tm, MAX_BLOCKS), 1)
    onehot = jnp.where(blk_lane == pos // SLC_LEN, 1.0, 0.0).astype(BF16)
    k = _dot(h, wk_ref[...]).astype(BF16)
    for gi in range(ng):
        ks_pos = jnp.concatenate([k[:, gi * dh:(gi + 1) * dh], terms], axis=1)
        ks_ref[gi] = jnp.concatenate([onehot, ks_pos], axis=1)
        kw_ref[gi] = jnp.concatenate([k[:, (ng + gi) * dh:(ng + gi + 1) * dh], terms], axis=1)
    vt = _dot_nt(wv_ref[...], h).astype(BF16)
    pad_row = lax.broadcasted_iota(jnp.int32, (V_PAD, KEY_TILE), 0)
    ones_rows = jnp.where(pad_row == 0, 1.0, 0.0).astype(BF16)
    for gi in range(ng):
        for c in range(tm // KEY_TILE):
            cols = slice(c * KEY_TILE, (c + 1) * KEY_TILE)
            vst_ref[gi, c] = jnp.concatenate([vt[gi * dh:(gi + 1) * dh, cols], ones_rows], axis=0)
            vwt_ref[gi, c] = jnp.concatenate(
                [vt[(ng + gi) * dh:(ng + gi + 1) * dh, cols], ones_rows], axis=0)


def _nsa_in(x, g, wq_t, wc, wk, wv_t, wg_t, bg_t):
    bsz, s, d = x.shape
    ng, dh, kt = N_KV_GROUPS, HEAD_DIM, KEY_TILE
    per = ROW_TILE // kt
    return pl.pallas_call(
        _nsa_in_kernel,
        grid=(bsz, s // ROW_TILE),
        in_specs=[pl.BlockSpec((None, ROW_TILE, d), lambda b, i: (b, i, 0)),
                  _resident(g.shape), _resident(wq_t.shape), _resident(wc.shape),
                  _resident(wk.shape), _resident(wv_t.shape), _resident(wg_t.shape),
                  _resident(bg_t.shape)],
        out_specs=[
            pl.BlockSpec((None, wq_t.shape[0], ROW_TILE), lambda b, i: (b, 0, i)),
            pl.BlockSpec((None, ROW_TILE, wc.shape[1]), lambda b, i: (b, i, 0)),
            pl.BlockSpec((None, ng, ROW_TILE, dh + MAX_BLOCKS + AUX_COLS), lambda b, i: (b, 0, i, 0)),
            pl.BlockSpec((None, ng, ROW_TILE, dh + AUX_COLS), lambda b, i: (b, 0, i, 0)),
            pl.BlockSpec((None, ng, per, dh + V_PAD, kt), lambda b, i: (b, 0, i, 0, 0)),
            pl.BlockSpec((None, ng, per, dh + V_PAD, kt), lambda b, i: (b, 0, i, 0, 0)),
            pl.BlockSpec((None, wg_t.shape[0], ROW_TILE), lambda b, i: (b, 0, i)),
        ],
        out_shape=[
            jax.ShapeDtypeStruct((bsz, wq_t.shape[0], s), BF16),
            jax.ShapeDtypeStruct((bsz, s, wc.shape[1]), F32),
            jax.ShapeDtypeStruct((bsz, ng, s, dh + MAX_BLOCKS + AUX_COLS), BF16),
            jax.ShapeDtypeStruct((bsz, ng, s, dh + AUX_COLS), BF16),
            jax.ShapeDtypeStruct((bsz, ng, s // kt, dh + V_PAD, kt), BF16),
            jax.ShapeDtypeStruct((bsz, ng, s // kt, dh + V_PAD, kt), BF16),
            jax.ShapeDtypeStruct((bsz, wg_t.shape[0], s), F32),
        ],
        compiler_params=_cparams("parallel", "parallel"),
        name="nsa_in",
    )(x, g, wq_t, wc, wk, wv_t, wg_t, bg_t)


def _compress_kernel(zk_ref, zv_ref, pe_ref, w1_ref, w2_ref, kc_ref, vct_ref):
    nchunk = kc_ref.shape[1]
    dh = HEAD_DIM

    def pre_act(z_ref, t):
        acc = [jnp.zeros((nchunk, LANES), F32) for _ in range(CMP_LEN // CMP_STRIDE)]
        for p in range(CMP_STRIDE):
            zp = z_ref[pl.ds(p, nchunk, stride=CMP_STRIDE), :]
            for r in range(len(acc)):
                pp = r * CMP_STRIDE + p
                acc[r] = acc[r] + _dot((zp + pe_ref[t, pp:pp + 1, :]).astype(BF16), w1_ref[t, pp])
        return _gelu_tanh(acc[0] + pltpu.roll(acc[1], nchunk - 1, axis=0)).astype(BF16)

    kc = _dot(pre_act(zk_ref, 0), w2_ref[0]).astype(BF16)
    cmp_end = lax.broadcasted_iota(jnp.int32, (nchunk, 1), 0) * CMP_STRIDE + (CMP_LEN - 1)
    terms = _pos_terms(cmp_end)
    for gi in range(kc_ref.shape[0]):
        kc_ref[gi] = jnp.concatenate([kc[:, gi * dh:(gi + 1) * dh], terms], axis=1)
    vct = _dot_nt(w2_ref[1], pre_act(zv_ref, 1)).astype(BF16)
    for gi in range(vct_ref.shape[0]):
        vct_ref[gi] = vct[gi * dh:(gi + 1) * dh, :]


def _compress(kvc, pe, w1, w2):
    bsz, s, _ = kvc.shape
    nchunk = s // CMP_STRIDE
    halves = N_KV_GROUPS // CMP_GROUPS
    return pl.pallas_call(
        _compress_kernel,
        grid=(bsz, halves),
        in_specs=[
            pl.BlockSpec((None, s, LANES), lambda b, c: (b, 0, c)),
            pl.BlockSpec((None, s, LANES), lambda b, c: (b, 0, halves + c)),
            _resident(pe.shape), _resident(w1.shape), _resident(w2.shape),
        ],
        out_specs=[
            pl.BlockSpec((None, CMP_GROUPS, nchunk, HEAD_DIM + AUX_COLS), lambda b, c: (b, c, 0, 0)),
            pl.BlockSpec((None, CMP_GROUPS, HEAD_DIM, nchunk), lambda b, c: (b, c, 0, 0)),
        ],
        out_shape=[
            jax.ShapeDtypeStruct((bsz, N_KV_GROUPS, nchunk, HEAD_DIM + AUX_COLS), BF16),
            jax.ShapeDtypeStruct((bsz, N_KV_GROUPS, HEAD_DIM, nchunk), BF16),
        ],
        compiler_params=_cparams("parallel", "parallel"),
        name="nsa_compress",
    )(kvc, kvc, pe, w1, w2)


def _split3(x):
    hi = x.astype(BF16)
    r = x - hi.astype(F32)
    mid = r.astype(BF16)
    lo = (r - mid.astype(F32)).astype(BF16)
    return hi, mid, lo


def _key_tile(k_ref, p):
    return k_ref[pl.ds(pl.multiple_of(p * KEY_TILE, KEY_TILE), KEY_TILE), :]


def _col_max(scores, init=None):
    parts = [jnp.max(s, axis=0, keepdims=True) for s in scores]
    return functools.reduce(jnp.maximum, parts if init is None else [init] + parts)


def _scores(k_ref, tiles, q_aug, masks):
    out = []
    for p, mask in zip(tiles, masks):
        s = _dot(_key_tile(k_ref, p), q_aug)
        if mask is None:
            out.append(s)
        elif mask.dtype == jnp.bool_:
            out.append(jnp.where(mask, s, NEG_INF))
        else:
            out.append(s + mask)
    return out


def _weighted_values(vt_ref, tiles, scores, m):
    acc = None
    for i in range(0, len(tiles), 2):
        vt = jnp.concatenate([vt_ref[p] for p in tiles[i:i + 2]], axis=1)
        e = jnp.concatenate([jnp.exp2((s - m).astype(BF16)) for s in scores[i:i + 2]], axis=0)
        pv = _dot(vt, e)
        acc = pv if acc is None else acc + pv
    return acc


def _normalized(acc):
    dh = HEAD_DIM
    return acc[:dh] * (1.0 / jnp.maximum(acc[dh:dh + 1], 1e-30))


def _attn_kernel(qt_ref, gt_ref, kc_ref, vct_ref, ks_ref, vst_ref, kw_ref, vwt_ref, cmap_ref,
                 slope_ref, o_ref, list_ref, ms_ref, as_ref, *, n_sel):
    hg, dh, tq, tk = HEADS_PER_GROUP, HEAD_DIM, Q_TILE, KEY_TILE
    ncol = hg * tq
    nkt = vst_ref.shape[0] - 1
    qb = pl.program_id(2)
    t0 = qb * tq
    heads = range(hg)
    t_col = t0 + lax.broadcasted_iota(jnp.int32, (1, tq), 1)
    qt = qt_ref[...]
    slope_rows = slope_ref[...]
    q_pos = [jnp.concatenate([qt[h * dh:(h + 1) * dh, :], slope_rows[:, h * tq:(h + 1) * tq]], axis=0)
             for h in heads]

    nc = kc_ref.shape[0]
    cmp_end = lax.broadcasted_iota(jnp.int32, (nc, 1), 0) * CMP_STRIDE + (CMP_LEN - 1)
    m_c = cmp_end <= t_col
    kc, vct = kc_ref[...], vct_ref[...]
    o_c, p_sum = [], None
    s_c_all = [_dot(kc, q_pos[h]) for h in heads]
    for h in heads:
        s_c = jnp.where(m_c, s_c_all[h], NEG_INF)
        p_c = jnp.where(m_c, jnp.exp2(s_c - jnp.max(s_c, axis=0, keepdims=True)), 0.0)
        p_c = p_c * (1.0 / jnp.maximum(jnp.sum(p_c, axis=0, keepdims=True), 1e-30))
        o_c.append(_dot(vct, p_c.astype(BF16)))
        p_sum = p_c if p_sum is None else p_sum + p_c

    nb = cmap_ref.shape[0]
    cmap = cmap_ref[...]
    imp = sum(_dot(cmap, part) for part in _split3(p_sum))
    blk = lax.broadcasted_iota(jnp.int32, (nb, tq), 0)
    cur = (t0 + lax.broadcasted_iota(jnp.int32, (1, tq), 1)) // SLC_LEN
    forced = (blk == 0) | (blk == cur) | (blk == cur - 1)
    visible = blk <= cur
    sel = jnp.where(forced & visible, 1.0, 0.0)
    score = jnp.where(visible, jnp.where(forced, -jnp.inf, imp), -1.0)
    blk_f = blk.astype(F32)
    for _ in range(n_sel - N_FORCED):
        best = jnp.max(score, axis=0, keepdims=True)
        first = jnp.min(jnp.where(score == best, blk_f, float(nb)), axis=0, keepdims=True)
        pick = blk_f == first
        sel = jnp.where(pick, 1.0, sel)
        score = jnp.where(pick, -jnp.inf, score)
    sel = jnp.where(visible, sel, 0.0)

    cnt = _dot_nt(jnp.ones((8, tq), BF16), sel.astype(BF16))
    blk_any = jnp.where(cnt > 0.0, 1.0, 0.0).astype(BF16)
    r_i = lax.broadcasted_iota(jnp.int32, (nb, LANES), 0)
    c_i = lax.broadcasted_iota(jnp.int32, (nb, LANES), 1)
    bit = jnp.left_shift(1, r_i % FLAG_BITS).astype(F32)
    words = _dot(blk_any, jnp.where(r_i // FLAG_BITS == c_i, bit, 0.0).astype(BF16))
    words = [words[0, w].astype(jnp.int32) for w in range(nb // FLAG_BITS)]

    per_tile = tk // SLC_LEN
    n_w, n_q = WINDOW // tk, tq // tk
    n_before = t0 // tk
    n = jnp.int32(0)
    for p in range(nkt - n_q):
        word = words[p * per_tile // FLAG_BITS]
        hit = (jnp.right_shift(word, p * per_tile % FLAG_BITS) & (2 ** per_tile - 1)) != 0
        list_ref[n] = jnp.int32(p)
        n = n + (hit & (p < n_before)).astype(jnp.int32)
    for j in range(SEL_CHUNK - 1):
        list_ref[n + j] = jnp.int32(nkt)

    key_row = lax.broadcasted_iota(jnp.int32, (tk, 1), 0)
    w_tiles, w_masks = [], []
    for j in range(n_w + n_q):
        p = n_before - n_w + j
        exists = p >= 0
        p = jnp.maximum(p, 0)
        w_tiles.append(p)
        if j < n_q:
            pos = jnp.where(exists, p * tk, -FAR_POSITION) + key_row
            w_masks.append(t_col - pos < WINDOW)
        elif j < n_w:
            w_masks.append(jnp.where(exists, 0.0, NEG_INF))
        else:
            w_masks.append(p * tk + key_row <= t_col)
    s_w = [_scores(kw_ref, w_tiles, q_pos[h], w_masks) for h in heads]
    o_w = [_normalized(_weighted_values(vwt_ref, w_tiles, s_w[h], _col_max(s_w[h]))) for h in heads]

    bias = jnp.where(sel > 0.0, 0.0, -MASK_BIAS).astype(BF16)
    q_sel = [jnp.concatenate([bias, q_pos[h]], axis=0) for h in heads]
    d_tiles = [n_before + j for j in range(n_q)]
    d_masks = [p * tk + key_row <= t_col for p in d_tiles]
    s_d = [_scores(ks_ref, d_tiles, q_sel[h], d_masks) for h in heads]
    for h in heads:
        cols = slice(h * tq, (h + 1) * tq)
        m_d = _col_max(s_d[h])
        ms_ref[:, cols] = m_d
        as_ref[:, cols] = _weighted_values(vst_ref, d_tiles, s_d[h], m_d)

    def sel_chunk(c, carry):
        tiles = [list_ref[c * SEL_CHUNK + j] for j in range(SEL_CHUNK)]
        s_s = [_scores(ks_ref, tiles, q_sel[h], [None] * SEL_CHUNK) for h in heads]
        for h in heads:
            cols = slice(h * tq, (h + 1) * tq)
            m_prev = ms_ref[:, cols]
            m_new = _col_max(s_s[h], m_prev)
            ms_ref[:, cols] = m_new
            as_ref[:, cols] = (jnp.exp2(m_prev - m_new) * as_ref[:, cols]
                               + _weighted_values(vst_ref, tiles, s_s[h], m_new))
        return carry

    lax.fori_loop(0, (n + SEL_CHUNK - 1) // SEL_CHUNK, sel_chunk, 0)

    gt = gt_ref[...]
    out = []
    for h in heads:
        r = 3 * h
        o_s = _normalized(as_ref[:, h * tq:(h + 1) * tq])
        o_h = gt[r:r + 1] * o_c[h] + gt[r + 1:r + 2] * o_s + gt[r + 2:r + 3] * o_w[h]
        out.append(o_h.T)
    o_ref[...] = jnp.concatenate(out, axis=1).astype(o_ref.dtype)


def _attend(qt, gates_t, kc, vct, ks, vst, kw, vwt, cmap, slopes):
    bsz, _, s = qt.shape
    hg, dh, tq = HEADS_PER_GROUP, HEAD_DIM, Q_TILE
    nchunk = kc.shape[2]
    n_sel = min(N_SELECT, s // SLC_LEN)
    nkt = s // KEY_TILE
    assert tq % KEY_TILE == 0 and tq <= WINDOW and n_sel >= N_FORCED
    per_bg = lambda *tail: pl.BlockSpec((None, None) + tail, lambda b, g, i: (b, g) + (0,) * len(tail))
    null_k = jnp.zeros((KEY_TILE, ks.shape[-1]), BF16).at[:, MAX_BLOCKS + dh + NULL_COL].set(1.0)
    ks = jnp.concatenate([ks, jnp.broadcast_to(null_k, ks.shape[:2] + null_k.shape)], axis=2)
    vst = jnp.concatenate([vst, jnp.zeros(vst.shape[:2] + (1,) + vst.shape[3:], BF16)], axis=2)
    return pl.pallas_call(
        functools.partial(_attn_kernel, n_sel=n_sel),
        grid=(bsz, N_KV_GROUPS, s // tq),
        in_specs=[
            pl.BlockSpec((None, hg * dh, tq), lambda b, g, i: (b, g, i)),
            pl.BlockSpec((None, GATE_ROWS, tq), lambda b, g, i: (b, g, i)),
            per_bg(nchunk, dh + AUX_COLS), per_bg(dh, nchunk),
            per_bg(*ks.shape[2:]), per_bg(*vst.shape[2:]),
            per_bg(*kw.shape[2:]), per_bg(*vwt.shape[2:]),
            pl.BlockSpec(cmap.shape, lambda b, g, i: (0, 0)),
            pl.BlockSpec((None, AUX_COLS, hg * tq), lambda b, g, i: (g, 0, 0)),
        ],
        out_specs=pl.BlockSpec((None, tq, hg * dh), lambda b, g, i: (b, i, g)),
        out_shape=jax.ShapeDtypeStruct((bsz, s, N_HEADS * dh), BF16),
        scratch_shapes=[pltpu.SMEM((nkt + SEL_CHUNK,), jnp.int32),
                        pltpu.VMEM((1, hg * tq), F32), pltpu.VMEM((dh + V_PAD, hg * tq), F32)],
        compiler_params=_cparams("parallel", "parallel", "arbitrary"),
        name="nsa_attend",
    )(qt, gates_t, kc, vct, ks, vst, kw, vwt, cmap, slopes)


def _out_proj_kernel(o_ref, w_ref, x_ref, y_ref):
    y_ref[...] = x_ref[...] + _dot(o_ref[...], w_ref[...])


def _out_proj(o2, w, x2):
    n, d = x2.shape
    return pl.pallas_call(
        _out_proj_kernel,
        grid=(n // ROW_TILE,),
        in_specs=[pl.BlockSpec((ROW_TILE, o2.shape[1]), lambda i: (i, 0)), _resident(w.shape),
                  pl.BlockSpec((ROW_TILE, d), lambda i: (i, 0))],
        out_specs=pl.BlockSpec((ROW_TILE, d), lambda i: (i, 0)),
        out_shape=jax.ShapeDtypeStruct((n, d), F32),
        compiler_params=_cparams("parallel"),
        name="nsa_out",
    )(o2, w, x2)


def _mlp_kernel(x_ref, g_ref, wu_ref, wd_ref, gf_ref, o_ref, *, final_norm):
    x = x_ref[...]
    h = _rmsnorm(x, g_ref[...]).astype(BF16)
    acc = x
    for c in range(wu_ref.shape[1] // FF_CHUNK):
        cols = slice(c * FF_CHUNK, (c + 1) * FF_CHUNK)
        u = jnp.maximum(_dot(h, wu_ref[:, cols]), 0.0)
        acc = acc + _dot((u * u).astype(BF16), wd_ref[cols, :])
    o_ref[...] = _rmsnorm(acc, gf_ref[...]) if final_norm else acc


def _mlp(x2, g, w_up, w_down, g_final, final_norm):
    n, d = x2.shape
    return pl.pallas_call(
        functools.partial(_mlp_kernel, final_norm=final_norm),
        grid=(n // ROW_TILE,),
        in_specs=[pl.BlockSpec((ROW_TILE, d), lambda i: (i, 0)), _resident(g.shape),
                  _resident(w_up.shape), _resident(w_down.shape), _resident(g_final.shape)],
        out_specs=pl.BlockSpec((ROW_TILE, d), lambda i: (i, 0)),
        out_shape=jax.ShapeDtypeStruct((n, d), F32),
        compiler_params=_cparams("parallel"),
        name="mlp",
    )(x2, g, w_up, w_down, g_final)


def _block_diag(w):
    eye = jnp.eye(CMP_GROUPS, dtype=w.dtype)
    out = jnp.einsum("gh,...ij->...gihj", eye, w)
    return out.reshape(*w.shape[:-2], CMP_GROUPS * w.shape[-2], CMP_GROUPS * w.shape[-1])


def _cmp_to_slc_map(nchunk, nb):
    cs = jnp.arange(nchunk) * CMP_STRIDE
    ss = jnp.arange(MAX_BLOCKS) * SLC_LEN
    ov = jnp.minimum(cs[None, :] + CMP_LEN, ss[:, None] + SLC_LEN) - jnp.maximum(cs[None, :], ss[:, None])
    ov = jnp.where(jnp.arange(MAX_BLOCKS)[:, None] < nb, ov, 0)
    return (jnp.maximum(ov, 0) / CMP_STRIDE).astype(BF16)


def _alibi_slope_rows():
    h = jnp.arange(1, N_HEADS + 1, dtype=F32)
    slope = jnp.exp2(-ALIBI_MAX * h / N_HEADS) * LOG2E
    hi = slope.astype(BF16)
    lo = (slope - hi.astype(F32)).astype(BF16)
    null = jnp.full_like(hi, -MASK_BIAS)
    rows = jnp.stack([hi, hi, lo, lo, null], axis=0)
    rows = jnp.pad(rows, ((0, AUX_COLS - rows.shape[0]), (0, 0)))
    rows = rows.reshape(AUX_COLS, N_KV_GROUPS, HEADS_PER_GROUP).transpose(1, 0, 2)
    return jnp.repeat(rows, Q_TILE, axis=2)


def _rg_layer(x, g, w_in, conv_w, conv_b, w_a, b_a, w_x, b_x, lam, w_out):
    bsz, s, d = x.shape
    row = lambda v: v.reshape(1, -1)
    y, xp = _rg_in(x.reshape(bsz * s, d), row(g), w_in.astype(BF16))
    return _rg_scan(xp.reshape(bsz, s, d), y.reshape(bsz, s, d), x, conv_w, row(conv_b),
                    w_a.astype(BF16), row(b_a), w_x.astype(BF16), row(b_x), row(lam),
                    w_out.astype(BF16))


def _nsa_layer(x, g, w_in, b_gate, pe_k, pe_v, w1_k, w2_k, w1_v, w2_v, w_out):
    bsz, s, d = x.shape
    assert s // SLC_LEN <= MAX_BLOCKS and s % ROW_TILE == 0
    q_cols = N_HEADS * HEAD_DIM
    kv_cols = N_KV_GROUPS * HEAD_DIM
    kv = w_in[:, q_cols:q_cols + 6 * kv_cols].reshape(d, 6, kv_cols)
    wq_t = w_in[:, :q_cols].T.astype(BF16)
    wc = kv[:, 0:2].reshape(d, 2 * kv_cols).astype(BF16)
    wk = jnp.concatenate([kv[:, 2], kv[:, 4]], axis=1).astype(BF16)
    wv_t = jnp.concatenate([kv[:, 3], kv[:, 5]], axis=1).T.astype(BF16)
    per_group = 3 * HEADS_PER_GROUP
    wg = w_in[:, q_cols + 6 * kv_cols:].reshape(d, N_KV_GROUPS, per_group)
    wg = jnp.pad(wg, ((0, 0), (0, 0), (0, GATE_ROWS - per_group))).reshape(d, -1)
    bg = jnp.pad(b_gate.reshape(N_KV_GROUPS, per_group), ((0, 0), (0, GATE_ROWS - per_group)))
    qt, kvc, ks, kw, vst, vwt, gates_t = _nsa_in(
        x, g.reshape(1, d), wq_t, wc, wk, wv_t, wg.T.astype(BF16), bg.reshape(-1, 1))

    pe = jnp.stack([jnp.tile(pe_k, (1, CMP_GROUPS)), jnp.tile(pe_v, (1, CMP_GROUPS))])
    w1 = jnp.stack([w1_k, w1_v]).reshape(2, CMP_LEN, HEAD_DIM, HEAD_DIM)
    w2 = jnp.stack([_block_diag(w2_k), _block_diag(w2_v).T])
    kc, vct = _compress(kvc, pe, _block_diag(w1).astype(BF16), w2.astype(BF16))

    cmap = _cmp_to_slc_map(s // CMP_STRIDE, s // SLC_LEN)
    o = _attend(qt, gates_t, kc, vct, ks, vst, kw, vwt, cmap, _alibi_slope_rows())
    return _out_proj(o.reshape(bsz * s, -1), w_out.astype(BF16), x.reshape(bsz * s, d))


def kernel(x, norm_mix, norm_ffn, norm_final, rg_w_in, rg_conv_w, rg_conv_b, rg_w_a, rg_b_a, rg_w_x, rg_b_x, rg_lambda, rg_w_out, nsa_w_in, nsa_b_gate, nsa_pe_k, nsa_pe_v, nsa_w1_k, nsa_w2_k, nsa_w1_v, nsa_w2_v, nsa_w_out, mlp_w_up, mlp_w_down):
    bsz, s, d = x.shape
    depth = norm_mix.shape[0]
    for i in range(depth):
        j = i // 2
        if i % 2 == 0:
            x = _rg_layer(x, norm_mix[i], rg_w_in[j], rg_conv_w[j], rg_conv_b[j], rg_w_a[j],
                          rg_b_a[j], rg_w_x[j], rg_b_x[j], rg_lambda[j], rg_w_out[j])
        else:
            x = _nsa_layer(x, norm_mix[i], nsa_w_in[j], nsa_b_gate[j], nsa_pe_k[j], nsa_pe_v[j],
                           nsa_w1_k[j], nsa_w2_k[j], nsa_w1_v[j], nsa_w2_v[j], nsa_w_out[j])
            x = x.reshape(bsz, s, d)
        x = _mlp(x.reshape(bsz * s, d), norm_ffn[i].reshape(1, d), mlp_w_up[i].astype(BF16),
                 mlp_w_down[i].astype(BF16), norm_final.reshape(1, d),
                 final_norm=(i == depth - 1)).reshape(bsz, s, d)
    return x
```

```python
import functools
import math

import jax
import jax.numpy as jnp
from jax import lax
from jax.experimental import pallas as pl
from jax.experimental.pallas import tpu as pltpu

F32 = jnp.float32
BF16 = jnp.bfloat16

RMS_EPS = 1e-6
RG_BLOCKS = 4
CONV_W = 4
RG_C = 8.0
N_HEADS = 16
HEAD_DIM = 64
N_KV_GROUPS = 4
HEADS_PER_GROUP = N_HEADS // N_KV_GROUPS
CMP_LEN = 32
CMP_STRIDE = 16
SLC_LEN = 64
N_SELECT = 16
WINDOW = 512
ALIBI_MAX = 8.0
NEG_INF = -1e30
FORCE_BONUS = 1e4

VMEM_LIMIT_BYTES = 56 * 1024 * 1024
LANES = 128
SUBLANES = 8
CMP_GROUPS = LANES // HEAD_DIM

ROW_TILE = 512
SCAN_TILE = 256
Q_TILE = 256
KEY_TILE = 128
FF_CHUNK = 1024

MAX_BLOCKS = 128
AUX_COLS = 64
GATE_ROWS = 16
MASK_BIAS = 2.0 ** 100
SOFTMAX_FLOOR = -1e25
LEVEL_BLOCKS = 32
NULL_COL = 4
V_PAD = 16
LOG2E = math.log2(math.e)
FLAG_BITS = 16
N_FORCED = 3
SEL_CHUNK = 4
FAR_POSITION = 2 ** 30


def _cparams(*sem):
    return pltpu.CompilerParams(dimension_semantics=sem, vmem_limit_bytes=VMEM_LIMIT_BYTES)


def _resident(shape):
    nd = len(shape)
    return pl.BlockSpec(shape, lambda *_: (0,) * nd, pipeline_mode=pl.Buffered(1))


def _rmsnorm(x, g):
    y = x * lax.rsqrt(jnp.mean(x * x, axis=-1, keepdims=True) + RMS_EPS)
    return y * g


def _gelu_tanh(x):
    return jax.nn.gelu(x, approximate=True)


def _dot(a, b):
    return jnp.dot(a, b, preferred_element_type=F32)


def _dot_nt(a, b):
    return lax.dot_general(a, b, (((1,), (1,)), ((), ())), preferred_element_type=F32)


def _sigmoid(x):
    return 0.5 * jnp.tanh(0.5 * x) + 0.5


def _mlp_rows(x, g_ref, wu_ref, wd_ref):
    h = _rmsnorm(x, g_ref[...]).astype(BF16)
    acc = x
    for c in range(wu_ref.shape[1] // FF_CHUNK):
        cols = slice(c * FF_CHUNK, (c + 1) * FF_CHUNK)
        u = jnp.maximum(_dot(h, wu_ref[:, cols]), 0.0)
        acc = acc + _dot((u * u).astype(BF16), wd_ref[cols, :])
    return acc


def _rg_layer_kernel(x_ref, g_ref, win_ref, cw_ref, cb_ref, wa_ref, ba_ref, wx_ref, bx_ref,
                     lam_ref, wo_ref, gm_ref, wu_ref, wd_ref, o_ref, tail_ref, h_ref, mix_ref):
    ts, d = x_ref.shape
    bw = d // RG_BLOCKS

    @pl.when(pl.program_id(1) == 0)
    def _():
        tail_ref[...] = jnp.zeros_like(tail_ref)
        h_ref[...] = jnp.zeros_like(h_ref)
        mix_ref[...] = jnp.zeros_like(mix_ref)

    x = x_ref[...]
    proj = _dot(_rmsnorm(x, g_ref[...]).astype(BF16), win_ref[...])
    y = _gelu_tanh(proj[:, :d])
    xp = proj[:, d:]
    ext = jnp.concatenate([tail_ref[...], xp], axis=0)
    cw = cw_ref[...]
    xb = cb_ref[...] + cw[CONV_W - 1:CONV_W] * xp
    for k in range(1, CONV_W):
        xb = xb + cw[CONV_W - 1 - k:CONV_W - k] * pltpu.roll(ext, k, axis=0)[8:]
    tail_ref[...] = xp[ts - 8:]

    xb16 = xb.astype(BF16)
    ra = jnp.concatenate(
        [_dot(xb16[:, n * bw:(n + 1) * bw], wa_ref[n]) for n in range(RG_BLOCKS)], axis=1)
    rx = jnp.concatenate(
        [_dot(xb16[:, n * bw:(n + 1) * bw], wx_ref[n]) for n in range(RG_BLOCKS)], axis=1)

    o_ref[...] = _mlp_rows(mix_ref[...], gm_ref, wu_ref, wd_ref)

    r = _sigmoid(ra + ba_ref[...])
    i = _sigmoid(rx + bx_ref[...])
    nl = -lam_ref[...]
    softplus = jnp.maximum(nl, 0.0) + jnp.log1p(jnp.exp(-jnp.abs(nl)))
    log_a = -RG_C * r * softplus
    a = jnp.exp(log_a)
    b = jnp.sqrt(-jnp.tanh(log_a) * (a * a + 1.0)) * (i * xb)

    groups = ts // SUBLANES
    a = a.reshape(groups, SUBLANES, d)
    b = b.reshape(groups, SUBLANES, d)
    sub = lax.broadcasted_iota(jnp.int32, a.shape, 1)
    k = 1
    while k < SUBLANES:
        b = b + a * jnp.where(sub >= k, pltpu.roll(b, k, axis=1), 0.0)
        a = a * jnp.where(sub >= k, pltpu.roll(a, k, axis=1), 1.0)
        k *= 2
    carry = h_ref[0:1, :]
    rows = []
    for gi in range(groups):
        rows.append(a[gi] * carry + b[gi])
        carry = rows[-1][SUBLANES - 1:SUBLANES, :]
    hs = jnp.concatenate(rows, axis=0)
    h_ref[...] = jnp.broadcast_to(carry, h_ref.shape)

    mix_ref[...] = x + _dot((hs * y).astype(BF16), wo_ref[...])


def _rg_block(x, g, w_in, conv_w, conv_b, w_a, b_a, w_x, b_x, lam, w_out, g_mlp, w_up, w_down):
    bsz, s, d = x.shape
    n = s // SCAN_TILE
    weights = (g, w_in, conv_w, conv_b, w_a, b_a, w_x, b_x, lam, w_out, g_mlp, w_up, w_down)
    return pl.pallas_call(
        _rg_layer_kernel,
        grid=(bsz, n + 1),
        in_specs=[pl.BlockSpec((None, SCAN_TILE, d), lambda b, i: (b, jnp.minimum(i, n - 1), 0))]
        + [_resident(w.shape) for w in weights],
        out_specs=pl.BlockSpec((None, SCAN_TILE, d), lambda b, i: (b, jnp.maximum(i - 1, 0), 0)),
        out_shape=jax.ShapeDtypeStruct((bsz, s, d), F32),
        scratch_shapes=[pltpu.VMEM((8, d), F32), pltpu.VMEM((8, d), F32),
                        pltpu.VMEM((SCAN_TILE, d), F32)],
        compiler_params=_cparams("parallel", "arbitrary"),
        name="rg_block",
    )(x, *weights)


def _pos_terms(pos):
    lane = lax.broadcasted_iota(jnp.int32, (pos.shape[0], AUX_COLS), 1)
    hi = ((pos // SLC_LEN) * SLC_LEN).astype(F32)
    lo = (pos % SLC_LEN).astype(F32)
    return jnp.where(lane < 4, jnp.where(lane % 2 == 0, hi, lo), 0.0).astype(BF16)


def _nsa_in_kernel(x_ref, g_ref, wq_ref, wc_ref, wk_ref, wv_ref, wg_ref, bg_ref,
                   qt_ref, kvc_ref, ks_ref, kw_ref, vst_ref, vwt_ref, gt_ref):
    tm = x_ref.shape[0]
    dh = HEAD_DIM
    ng = ks_ref.shape[0]
    h = _rmsnorm(x_ref[...], g_ref[...]).astype(BF16)
    qt_ref[...] = (_dot_nt(wq_ref[...], h) * (dh ** -0.5 * LOG2E)).astype(qt_ref.dtype)
    kvc_ref[...] = _dot(h, wc_ref[...])
    gt_ref[...] = jax.nn.sigmoid(_dot_nt(wg_ref[...], h) + bg_ref[...])

    pos = pl.program_id(1) * tm + lax.broadcasted_iota(jnp.int32, (tm, 1), 0)
    terms = _pos_terms(pos)
    blk_lane = lax.broadcasted_iota(jnp.int32, (tm, MAX_BLOCKS), 1)
    onehot = jnp.where(blk_lane == pos // SLC_LEN, 1.0, 0.0).astype(BF16)
    k = _dot(h, wk_ref[...]).astype(BF16)
    for gi in range(ng):
        ks_pos = jnp.concatenate([k[:, gi * dh:(gi + 1) * dh], terms], axis=1)
        ks_ref[gi] = jnp.concatenate([onehot, ks_pos], axis=1)
        kw_ref[gi] = jnp.concatenate([k[:, (ng + gi) * dh:(ng + gi + 1) * dh], terms], axis=1)
    vt = _dot_nt(wv_ref[...], h).astype(BF16)
    pad_row = lax.broadcasted_iota(jnp.int32, (V_PAD, KEY_TILE), 0)
    ones_rows = jnp.where(pad_row == 0, 1.0, 0.0).astype(BF16)
    for gi in range(ng):
        for c in range(tm // KEY_TILE):
            cols = slice(c * KEY_TILE, (c + 1) * KEY_TILE)
            vst_ref[gi, c] = jnp.concatenate([vt[gi * dh:(gi + 1) * dh, cols], ones_rows], axis=0)
            vwt_ref[gi, c] = jnp.concatenate(
                [vt[(ng + gi) * dh:(ng + gi + 1) * dh, cols], ones_rows], axis=0)


def _nsa_in(x, g, wq_t, wc, wk, wv_t, wg_t, bg_t):
    bsz, s, d = x.shape
    ng, dh, kt = N_KV_GROUPS, HEAD_DIM, KEY_TILE
    per = ROW_TILE // kt
    return pl.pallas_call(
        _nsa_in_kernel,
        grid=(bsz, s // ROW_TILE),
        in_specs=[pl.BlockSpec((None, ROW_TILE, d), lambda b, i: (b, i, 0)),
                  _resident(g.shape), _resident(wq_t.shape), _resident(wc.shape),
                  _resident(wk.shape), _resident(wv_t.shape), _resident(wg_t.shape),
                  _resident(bg_t.shape)],
        out_specs=[
            pl.BlockSpec((None, wq_t.shape[0], ROW_TILE), lambda b, i: (b, 0, i)),
            pl.BlockSpec((None, ROW_TILE, wc.shape[1]), lambda b, i: (b, i, 0)),
            pl.BlockSpec((None, ng, ROW_TILE, dh + MAX_BLOCKS + AUX_COLS), lambda b, i: (b, 0, i, 0)),
            pl.BlockSpec((None, ng, ROW_TILE, dh + AUX_COLS), lambda b, i: (b, 0, i, 0)),
            pl.BlockSpec((None, ng, per, dh + V_PAD, kt), lambda b, i: (b, 0, i, 0, 0)),
            pl.BlockSpec((None, ng, per, dh + V_PAD, kt), lambda b, i: (b, 0, i, 0, 0)),
            pl.BlockSpec((None, wg_t.shape[0], ROW_TILE), lambda b, i: (b, 0, i)),
        ],
        out_shape=[
            jax.ShapeDtypeStruct((bsz, wq_t.shape[0], s), BF16),
            jax.ShapeDtypeStruct((bsz, s, wc.shape[1]), F32),
            jax.ShapeDtypeStruct((bsz, ng, s, dh + MAX_BLOCKS + AUX_COLS), BF16),
            jax.ShapeDtypeStruct((bsz, ng, s, dh + AUX_COLS), BF16),
            jax.ShapeDtypeStruct((bsz, ng, s // kt, dh + V_PAD, kt), BF16),
            jax.ShapeDtypeStruct((bsz, ng, s // kt, dh + V_PAD, kt), BF16),
            jax.ShapeDtypeStruct((bsz, wg_t.shape[0], s), F32),
        ],
        compiler_params=_cparams("parallel", "parallel"),
        name="nsa_in",
    )(x, g, wq_t, wc, wk, wv_t, wg_t, bg_t)


def _compress_kernel(zk_ref, zv_ref, pe_ref, w1_ref, w2_ref, kc_ref, vct_ref):
    nchunk = kc_ref.shape[1]
    dh = HEAD_DIM

    def pre_act(z_ref, t):
        acc = [jnp.zeros((nchunk, LANES), F32) for _ in range(CMP_LEN // CMP_STRIDE)]
        for p in range(CMP_STRIDE):
            zp = z_ref[pl.ds(p, nchunk, stride=CMP_STRIDE), :]
            for r in range(len(acc)):
                pp = r * CMP_STRIDE + p
                acc[r] = acc[r] + _dot((zp + pe_ref[t, pp:pp + 1, :]).astype(BF16), w1_ref[t, pp])
        return _gelu_tanh(acc[0] + pltpu.roll(acc[1], nchunk - 1, axis=0)).astype(BF16)

    kc = _dot(pre_act(zk_ref, 0), w2_ref[0]).astype(BF16)
    cmp_end = lax.broadcasted_iota(jnp.int32, (nchunk, 1), 0) * CMP_STRIDE + (CMP_LEN - 1)
    terms = _pos_terms(cmp_end)
    for gi in range(kc_ref.shape[0]):
        kc_ref[gi] = jnp.concatenate([kc[:, gi * dh:(gi + 1) * dh], terms], axis=1)
    vct = _dot_nt(w2_ref[1], pre_act(zv_ref, 1)).astype(BF16)
    for gi in range(vct_ref.shape[0]):
        vct_ref[gi] = vct[gi * dh:(gi + 1) * dh, :]


def _compress(kvc, pe, w1, w2):
    bsz, s, _ = kvc.shape
    nchunk = s // CMP_STRIDE
    halves = N_KV_GROUPS // CMP_GROUPS
    return pl.pallas_call(
        _compress_kernel,
        grid=(bsz, halves),
        in_specs=[
            pl.BlockSpec((None, s, LANES), lambda b, c: (b, 0, c)),
            pl.BlockSpec((None, s, LANES), lambda b, c: (b, 0, halves + c)),
            _resident(pe.shape), _resident(w1.shape), _resident(w2.shape),
        ],
        out_specs=[
            pl.BlockSpec((None, CMP_GROUPS, nchunk, HEAD_DIM + AUX_COLS), lambda b, c: (b, c, 0, 0)),
            pl.BlockSpec((None, CMP_GROUPS, HEAD_DIM, nchunk), lambda b, c: (b, c, 0, 0)),
        ],
        out_shape=[
            jax.ShapeDtypeStruct((bsz, N_KV_GROUPS, nchunk, HEAD_DIM + AUX_COLS), BF16),
            jax.ShapeDtypeStruct((bsz, N_KV_GROUPS, HEAD_DIM, nchunk), BF16),
        ],
        compiler_params=_cparams("parallel", "parallel"),
        name="nsa_compress",
    )(kvc, kvc, pe, w1, w2)


def _split3(x):
    hi = x.astype(BF16)
    r = x - hi.astype(F32)
    mid = r.astype(BF16)
    lo = (r - mid.astype(F32)).astype(BF16)
    return hi, mid, lo


def _key_tile(k_ref, p):
    return k_ref[pl.ds(pl.multiple_of(p * KEY_TILE, KEY_TILE), KEY_TILE), :]


def _col_max(scores, init=None):
    parts = [jnp.max(s, axis=0, keepdims=True) for s in scores]
    return functools.reduce(jnp.maximum, parts if init is None else [init] + parts)


def _scores(k_ref, tiles, q_aug, masks):
    out = []
    for p, mask in zip(tiles, masks):
        s = _dot(_key_tile(k_ref, p), q_aug)
        if mask is None:
            out.append(s)
        elif mask.dtype == jnp.bool_:
            out.append(jnp.where(mask, s, NEG_INF))
        else:
            out.append(s + mask)
    return out


def _weighted_values(vt_ref, tiles, scores, m):
    acc = None
    for i in range(0, len(tiles), 2):
        vt = jnp.concatenate([vt_ref[p] for p in tiles[i:i + 2]], axis=1)
        e = jnp.concatenate([jnp.exp2((s - m).astype(BF16)) for s in scores[i:i + 2]], axis=0)
        pv = _dot(vt, e)
        acc = pv if acc is None else acc + pv
    return acc


def _normalized(acc):
    dh = HEAD_DIM
    return acc[:dh] * (1.0 / jnp.maximum(acc[dh:dh + 1], 1e-30))


def _attn_kernel(qt_ref, gt_ref, kc_ref, vct_ref, ks_ref, vst_ref, kw_ref, vwt_ref, cmap_ref,
                 slope_ref, o_ref, list_ref, ms_ref, as_ref, *, n_sel):
    hg, dh, tq, tk = HEADS_PER_GROUP, HEAD_DIM, Q_TILE, KEY_TILE
    ncol = hg * tq
    nkt = vst_ref.shape[0] - 1
    qb = pl.program_id(2)
    t0 = qb * tq
    heads = range(hg)
    t_col = t0 + lax.broadcasted_iota(jnp.int32, (1, tq), 1)
    qt = qt_ref[...]
    slope_rows = slope_ref[...]
    q_pos = [jnp.concatenate([qt[h * dh:(h + 1) * dh, :], slope_rows[:, h * tq:(h + 1) * tq]], axis=0)
             for h in heads]

    nb = cmap_ref.shape[0]
    cur = t_col // SLC_LEN

    def compress_and_select(level):
        nb_v = LEVEL_BLOCKS * (level + 1)
        nc_v = nb_v * SLC_LEN // CMP_STRIDE
        cmp_end = lax.broadcasted_iota(jnp.int32, (nc_v, 1), 0) * CMP_STRIDE + (CMP_LEN - 1)
        m_c = cmp_end <= t_col
        kc, vct = kc_ref[0:nc_v, :], vct_ref[:, 0:nc_v]
        o_c, p_sum = [], None
        s_c_all = [_dot(kc, q_pos[h]) for h in heads]
        for h in heads:
            s_c = jnp.where(m_c, s_c_all[h], NEG_INF)
            m = jnp.maximum(jnp.max(s_c, axis=0, keepdims=True), SOFTMAX_FLOOR)
            p_c = jnp.exp2(s_c - m)
            p_c = p_c * (1.0 / jnp.maximum(jnp.sum(p_c, axis=0, keepdims=True), 1e-30))
            o_c.append(_dot(vct, p_c.astype(BF16)))
            p_sum = p_c if p_sum is None else p_sum + p_c

        cmap = cmap_ref[0:nb_v, 0:nc_v]
        imp = sum(_dot(cmap, part) for part in _split3(p_sum))
        blk = lax.broadcasted_iota(jnp.int32, (nb_v, tq), 0)
        forced = (blk == 0) | (blk == cur) | (blk == cur - 1)
        visible = blk <= cur
        score = jnp.where(visible, jnp.where(forced, -jnp.inf, imp), -1.0)
        blk_f = blk.astype(F32)
        for _ in range(n_sel - N_FORCED):
            best = jnp.max(score, axis=0, keepdims=True)
            first = jnp.min(jnp.where(score == best, blk_f, float(nb)), axis=0, keepdims=True)
            score = jnp.where(blk_f == first, -jnp.inf, score)
        sel = jnp.where(visible & (score == -jnp.inf), 1.0, 0.0)
        if nb_v < nb:
            sel = jnp.concatenate([sel, jnp.zeros((nb - nb_v, tq), F32)], axis=0)
        return jnp.concatenate(o_c, axis=0), sel

    n_levels = -(-(nkt * tk // SLC_LEN) // LEVEL_BLOCKS)
    level = (t0 + tq - 1) // (LEVEL_BLOCKS * SLC_LEN)
    o_c, sel = lax.switch(level, [functools.partial(compress_and_select, v) for v in range(n_levels)])
    o_c = [o_c[h * dh:(h + 1) * dh] for h in heads]

    cnt = _dot_nt(jnp.ones((8, tq), BF16), sel.astype(BF16))
    blk_any = jnp.where(cnt > 0.0, 1.0, 0.0).astype(BF16)
    r_i = lax.broadcasted_iota(jnp.int32, (nb, LANES), 0)
    c_i = lax.broadcasted_iota(jnp.int32, (nb, LANES), 1)
    bit = jnp.left_shift(1, r_i % FLAG_BITS).astype(F32)
    words = _dot(blk_any, jnp.where(r_i // FLAG_BITS == c_i, bit, 0.0).astype(BF16))
    words = [words[0, w].astype(jnp.int32) for w in range(nb // FLAG_BITS)]

    per_tile = tk // SLC_LEN
    n_w, n_q = WINDOW // tk, tq // tk
    n_before = t0 // tk
    n = jnp.int32(0)
    for p in range(nkt - n_q):
        word = words[p * per_tile // FLAG_BITS]
        hit = (jnp.right_shift(word, p * per_tile % FLAG_BITS) & (2 ** per_tile - 1)) != 0
        list_ref[n] = jnp.int32(p)
        n = n + (hit & (p < n_before)).astype(jnp.int32)
    for j in range(SEL_CHUNK - 1):
        list_ref[n + j] = jnp.int32(nkt)

    key_row = lax.broadcasted_iota(jnp.int32, (tk, 1), 0)
    w_tiles, w_masks = [], []
    for j in range(n_w + n_q):
        p = n_before - n_w + j
        exists = p >= 0
        p = jnp.maximum(p, 0)
        w_tiles.append(p)
        if j < n_q:
            pos = jnp.where(exists, p * tk, -FAR_POSITION) + key_row
            w_masks.append(t_col - pos < WINDOW)
        elif j < n_w:
            w_masks.append(jnp.where(exists, 0.0, NEG_INF))
        else:
            w_masks.append(p * tk + key_row <= t_col)
    s_w = [_scores(kw_ref, w_tiles, q_pos[h], w_masks) for h in heads]

    bias = jnp.where(sel > 0.0, 0.0, -MASK_BIAS).astype(BF16)
    q_sel = [jnp.concatenate([bias, q_pos[h]], axis=0) for h in heads]
    d_tiles = [n_before + j for j in range(n_q)]
    d_masks = [p * tk + key_row <= t_col for p in d_tiles]
    s_d = [_scores(ks_ref, d_tiles, q_sel[h], d_masks) for h in heads]
    o_w = [_normalized(_weighted_values(vwt_ref, w_tiles, s_w[h], _col_max(s_w[h]))) for h in heads]
    for h in heads:
        cols = slice(h * tq, (h + 1) * tq)
        m_d = _col_max(s_d[h])
        ms_ref[:, cols] = m_d
        as_ref[:, cols] = _weighted_values(vst_ref, d_tiles, s_d[h], m_d)

    def sel_chunk(c, carry):
        tiles = [list_ref[c * SEL_CHUNK + j] for j in range(SEL_CHUNK)]
        s_s = [_scores(ks_ref, tiles, q_sel[h], [None] * SEL_CHUNK) for h in heads]
        for h in heads:
            cols = slice(h * tq, (h + 1) * tq)
            m_prev = ms_ref[:, cols]
            m_new = _col_max(s_s[h], m_prev)
            ms_ref[:, cols] = m_new
            as_ref[:, cols] = (jnp.exp2(m_prev - m_new) * as_ref[:, cols]
                               + _weighted_values(vst_ref, tiles, s_s[h], m_new))
        return carry

    lax.fori_loop(0, (n + SEL_CHUNK - 1) // SEL_CHUNK, sel_chunk, 0)

    gt = gt_ref[...]
    out = []
    for h in heads:
        r = 3 * h
        o_s = _normalized(as_ref[:, h * tq:(h + 1) * tq])
        o_h = gt[r:r + 1] * o_c[h] + gt[r + 1:r + 2] * o_s + gt[r + 2:r + 3] * o_w[h]
        out.append(o_h.T)
    o_ref[...] = jnp.concatenate(out, axis=1).astype(o_ref.dtype)


def _attend(qt, gates_t, kc, vct, ks, vst, kw, vwt, cmap, slopes):
    bsz, _, s = qt.shape
    hg, dh, tq = HEADS_PER_GROUP, HEAD_DIM, Q_TILE
    nchunk = kc.shape[2]
    n_sel = min(N_SELECT, s // SLC_LEN)
    nkt = s // KEY_TILE
    assert tq % KEY_TILE == 0 and tq <= WINDOW and n_sel >= N_FORCED
    per_bg = lambda *tail: pl.BlockSpec((None, None) + tail, lambda b, g, i: (b, g) + (0,) * len(tail))
    null_k = jnp.zeros((KEY_TILE, ks.shape[-1]), BF16).at[:, MAX_BLOCKS + dh + NULL_COL].set(1.0)
    ks = jnp.concatenate([ks, jnp.broadcast_to(null_k, ks.shape[:2] + null_k.shape)], axis=2)
    vst = jnp.concatenate([vst, jnp.zeros(vst.shape[:2] + (1,) + vst.shape[3:], BF16)], axis=2)
    return pl.pallas_call(
        functools.partial(_attn_kernel, n_sel=n_sel),
        grid=(bsz, N_KV_GROUPS, s // tq),
        in_specs=[
            pl.BlockSpec((None, hg * dh, tq), lambda b, g, i: (b, g, i)),
            pl.BlockSpec((None, GATE_ROWS, tq), lambda b, g, i: (b, g, i)),
            per_bg(nchunk, dh + AUX_COLS), per_bg(dh, nchunk),
            per_bg(*ks.shape[2:]), per_bg(*vst.shape[2:]),
            per_bg(*kw.shape[2:]), per_bg(*vwt.shape[2:]),
            pl.BlockSpec(cmap.shape, lambda b, g, i: (0, 0)),
            pl.BlockSpec((None, AUX_COLS, hg * tq), lambda b, g, i: (g, 0, 0)),
        ],
        out_specs=pl.BlockSpec((None, tq, hg * dh), lambda b, g, i: (b, i, g)),
        out_shape=jax.ShapeDtypeStruct((bsz, s, N_HEADS * dh), BF16),
        scratch_shapes=[pltpu.SMEM((nkt + SEL_CHUNK,), jnp.int32),
                        pltpu.VMEM((1, hg * tq), F32), pltpu.VMEM((dh + V_PAD, hg * tq), F32)],
        compiler_params=_cparams("parallel", "parallel", "arbitrary"),
        name="nsa_attend",
    )(qt, gates_t, kc, vct, ks, vst, kw, vwt, cmap, slopes)


def _nsa_out_kernel(o_ref, wo_ref, x_ref, gm_ref, wu_ref, wd_ref, gf_ref, y_ref, *, final_norm):
    y = _mlp_rows(x_ref[...] + _dot(o_ref[...], wo_ref[...]), gm_ref, wu_ref, wd_ref)
    y_ref[...] = _rmsnorm(y, gf_ref[...]) if final_norm else y


def _nsa_out(o2, w_out, x2, g_mlp, w_up, w_down, g_final, final_norm):
    n, d = x2.shape
    weights = (g_mlp, w_up, w_down, g_final)
    return pl.pallas_call(
        functools.partial(_nsa_out_kernel, final_norm=final_norm),
        grid=(n // ROW_TILE,),
        in_specs=[pl.BlockSpec((ROW_TILE, o2.shape[1]), lambda i: (i, 0)), _resident(w_out.shape),
                  pl.BlockSpec((ROW_TILE, d), lambda i: (i, 0))]
        + [_resident(w.shape) for w in weights],
        out_specs=pl.BlockSpec((ROW_TILE, d), lambda i: (i, 0)),
        out_shape=jax.ShapeDtypeStruct((n, d), F32),
        compiler_params=_cparams("parallel"),
        name="nsa_out_mlp",
    )(o2, w_out, x2, *weights)


def _block_diag(w):
    eye = jnp.eye(CMP_GROUPS, dtype=w.dtype)
    out = jnp.einsum("gh,...ij->...gihj", eye, w)
    return out.reshape(*w.shape[:-2], CMP_GROUPS * w.shape[-2], CMP_GROUPS * w.shape[-1])


def _cmp_to_slc_map(nchunk, nb):
    cs = jnp.arange(nchunk) * CMP_STRIDE
    ss = jnp.arange(MAX_BLOCKS) * SLC_LEN
    ov = jnp.minimum(cs[None, :] + CMP_LEN, ss[:, None] + SLC_LEN) - jnp.maximum(cs[None, :], ss[:, None])
    ov = jnp.where(jnp.arange(MAX_BLOCKS)[:, None] < nb, ov, 0)
    return (jnp.maximum(ov, 0) / CMP_STRIDE).astype(BF16)


def _alibi_slope_rows():
    h = jnp.arange(1, N_HEADS + 1, dtype=F32)
    slope = jnp.exp2(-ALIBI_MAX * h / N_HEADS) * LOG2E
    hi = slope.astype(BF16)
    lo = (slope - hi.astype(F32)).astype(BF16)
    null = jnp.full_like(hi, -MASK_BIAS)
    rows = jnp.stack([hi, hi, lo, lo, null], axis=0)
    rows = jnp.pad(rows, ((0, AUX_COLS - rows.shape[0]), (0, 0)))
    rows = rows.reshape(AUX_COLS, N_KV_GROUPS, HEADS_PER_GROUP).transpose(1, 0, 2)
    return jnp.repeat(rows, Q_TILE, axis=2)


def _rg_layer(x, g, w_in, conv_w, conv_b, w_a, b_a, w_x, b_x, lam, w_out, g_mlp, w_up, w_down):
    row = lambda v: v.reshape(1, -1)
    return _rg_block(x, row(g), w_in.astype(BF16), conv_w, row(conv_b), w_a.astype(BF16), row(b_a),
                     w_x.astype(BF16), row(b_x), row(lam), w_out.astype(BF16),
                     row(g_mlp), w_up.astype(BF16), w_down.astype(BF16))


def _nsa_layer(x, g, w_in, b_gate, pe_k, pe_v, w1_k, w2_k, w1_v, w2_v, w_out,
               g_mlp, w_up, w_down, g_final, final_norm):
    bsz, s, d = x.shape
    assert s // SLC_LEN <= MAX_BLOCKS and s % ROW_TILE == 0
    q_cols = N_HEADS * HEAD_DIM
    kv_cols = N_KV_GROUPS * HEAD_DIM
    kv = w_in[:, q_cols:q_cols + 6 * kv_cols].reshape(d, 6, kv_cols)
    wq_t = w_in[:, :q_cols].T.astype(BF16)
    wc = kv[:, 0:2].reshape(d, 2 * kv_cols).astype(BF16)
    wk = jnp.concatenate([kv[:, 2], kv[:, 4]], axis=1).astype(BF16)
    wv_t = jnp.concatenate([kv[:, 3], kv[:, 5]], axis=1).T.astype(BF16)
    per_group = 3 * HEADS_PER_GROUP
    wg = w_in[:, q_cols + 6 * kv_cols:].reshape(d, N_KV_GROUPS, per_group)
    wg = jnp.pad(wg, ((0, 0), (0, 0), (0, GATE_ROWS - per_group))).reshape(d, -1)
    bg = jnp.pad(b_gate.reshape(N_KV_GROUPS, per_group), ((0, 0), (0, GATE_ROWS - per_group)))
    qt, kvc, ks, kw, vst, vwt, gates_t = _nsa_in(
        x, g.reshape(1, d), wq_t, wc, wk, wv_t, wg.T.astype(BF16), bg.reshape(-1, 1))

    pe = jnp.stack([jnp.tile(pe_k, (1, CMP_GROUPS)), jnp.tile(pe_v, (1, CMP_GROUPS))])
    w1 = jnp.stack([w1_k, w1_v]).reshape(2, CMP_LEN, HEAD_DIM, HEAD_DIM)
    w2 = jnp.stack([_block_diag(w2_k), _block_diag(w2_v).T])
    kc, vct = _compress(kvc, pe, _block_diag(w1).astype(BF16), w2.astype(BF16))

    cmap = _cmp_to_slc_map(s // CMP_STRIDE, s // SLC_LEN)
    o = _attend(qt, gates_t, kc, vct, ks, vst, kw, vwt, cmap, _alibi_slope_rows())
    y = _nsa_out(o.reshape(bsz * s, -1), w_out.astype(BF16), x.reshape(bsz * s, d),
                 g_mlp.reshape(1, d), w_up.astype(BF16), w_down.astype(BF16),
                 g_final.reshape(1, d), final_norm)
    return y.reshape(bsz, s, d)


def kernel(x, norm_mix, norm_ffn, norm_final, rg_w_in, rg_conv_w, rg_conv_b, rg_w_a, rg_b_a, rg_w_x, rg_b_x, rg_lambda, rg_w_out, nsa_w_in, nsa_b_gate, nsa_pe_k, nsa_pe_v, nsa_w1_k, nsa_w2_k, nsa_w1_v, nsa_w2_v, nsa_w_out, mlp_w_up, mlp_w_down):
    depth = norm_mix.shape[0]
    assert depth % 2 == 0
    for i in range(depth):
        j = i // 2
        if i % 2 == 0:
            x = _rg_layer(x, norm_mix[i], rg_w_in[j], rg_conv_w[j], rg_conv_b[j], rg_w_a[j],
                          rg_b_a[j], rg_w_x[j], rg_b_x[j], rg_lambda[j], rg_w_out[j],
                          norm_ffn[i], mlp_w_up[i], mlp_w_down[i])
        else:
            x = _nsa_layer(x, norm_mix[i], nsa_w_in[j], nsa_b_gate[j], nsa_pe_k[j], nsa_pe_v[j],
                           nsa_w1_k[j], nsa_w2_k[j], nsa_w1_v[j], nsa_w2_v[j], nsa_w_out[j],
                           norm_ffn[i], mlp_w_up[i], mlp_w_down[i], norm_final,
                           final_norm=(i == depth - 1))
    return x
```

```python
import functools
import math

import jax
import jax.numpy as jnp
from jax import lax
from jax.experimental import pallas as pl
from jax.experimental.pallas import tpu as pltpu

F32 = jnp.float32
BF16 = jnp.bfloat16

RMS_EPS = 1e-6
RG_BLOCKS = 4
CONV_W = 4
RG_C = 8.0
N_HEADS = 16
HEAD_DIM = 64
N_KV_GROUPS = 4
HEADS_PER_GROUP = N_HEADS // N_KV_GROUPS
CMP_LEN = 32
CMP_STRIDE = 16
SLC_LEN = 64
N_SELECT = 16
WINDOW = 512
ALIBI_MAX = 8.0
NEG_INF = -1e30
FORCE_BONUS = 1e4

VMEM_LIMIT_BYTES = 56 * 1024 * 1024
LANES = 128
SUBLANES = 8
CMP_GROUPS = LANES // HEAD_DIM

ROW_TILE = 512
SCAN_TILE = 256
Q_TILE = 256
KEY_TILE = 128
FF_CHUNK = 1024

MAX_BLOCKS = 128
AUX_COLS = 64
GATE_ROWS = 16
MASK_BIAS = 2.0 ** 100
SOFTMAX_FLOOR = -1e25
LEVEL_BLOCKS = 32
NULL_COL = 4
V_PAD = 16
LOG2E = math.log2(math.e)
FLAG_BITS = 16
N_FORCED = 3
SEL_CHUNK = 4
FAR_POSITION = 2 ** 30


def _cparams(*sem):
    return pltpu.CompilerParams(dimension_semantics=sem, vmem_limit_bytes=VMEM_LIMIT_BYTES)


def _resident(shape):
    nd = len(shape)
    return pl.BlockSpec(shape, lambda *_: (0,) * nd, pipeline_mode=pl.Buffered(1))


def _rmsnorm(x, g):
    y = x * lax.rsqrt(jnp.mean(x * x, axis=-1, keepdims=True) + RMS_EPS)
    return y * g


def _gelu_tanh(x):
    return jax.nn.gelu(x, approximate=True)


def _dot(a, b):
    return jnp.dot(a, b, preferred_element_type=F32)


def _dot_nt(a, b):
    return lax.dot_general(a, b, (((1,), (1,)), ((), ())), preferred_element_type=F32)


def _sigmoid(x):
    return 0.5 * jnp.tanh(0.5 * x) + 0.5


def _mlp_rows(x, g_ref, wu_ref, wd_ref):
    h = _rmsnorm(x, g_ref[...]).astype(BF16)
    acc = x
    for c in range(wu_ref.shape[1] // FF_CHUNK):
        cols = slice(c * FF_CHUNK, (c + 1) * FF_CHUNK)
        u = jnp.maximum(_dot(h, wu_ref[:, cols]), 0.0)
        acc = acc + _dot((u * u).astype(BF16), wd_ref[cols, :])
    return acc


def _rg_layer_kernel(x_ref, g_ref, win_ref, cw_ref, cb_ref, wa_ref, ba_ref, wx_ref, bx_ref,
                     lam_ref, wo_ref, gm_ref, wu_ref, wd_ref, o_ref, tail_ref, h_ref, mix_ref):
    ts, d = x_ref.shape
    bw = d // RG_BLOCKS

    @pl.when(pl.program_id(1) == 0)
    def _():
        tail_ref[...] = jnp.zeros_like(tail_ref)
        h_ref[...] = jnp.zeros_like(h_ref)
        mix_ref[...] = jnp.zeros_like(mix_ref)

    x = x_ref[...]
    proj = _dot(_rmsnorm(x, g_ref[...]).astype(BF16), win_ref[...])
    y = _gelu_tanh(proj[:, :d])
    xp = proj[:, d:]
    ext = jnp.concatenate([tail_ref[...], xp], axis=0)
    cw = cw_ref[...]
    xb = cb_ref[...] + cw[CONV_W - 1:CONV_W] * xp
    for k in range(1, CONV_W):
        xb = xb + cw[CONV_W - 1 - k:CONV_W - k] * pltpu.roll(ext, k, axis=0)[8:]
    tail_ref[...] = xp[ts - 8:]

    xb16 = xb.astype(BF16)
    ra = jnp.concatenate(
        [_dot(xb16[:, n * bw:(n + 1) * bw], wa_ref[n]) for n in range(RG_BLOCKS)], axis=1)
    rx = jnp.concatenate(
        [_dot(xb16[:, n * bw:(n + 1) * bw], wx_ref[n]) for n in range(RG_BLOCKS)], axis=1)

    o_ref[...] = _mlp_rows(mix_ref[...], gm_ref, wu_ref, wd_ref)

    r = _sigmoid(ra + ba_ref[...])
    i = _sigmoid(rx + bx_ref[...])
    nl = -lam_ref[...]
    softplus = jnp.maximum(nl, 0.0) + jnp.log1p(jnp.exp(-jnp.abs(nl)))
    log_a = -RG_C * r * softplus
    a = jnp.exp(log_a)
    b = jnp.sqrt(-jnp.tanh(log_a) * (a * a + 1.0)) * (i * xb)

    groups = ts // SUBLANES
    a = a.reshape(groups, SUBLANES, d)
    b = b.reshape(groups, SUBLANES, d)
    sub = lax.broadcasted_iota(jnp.int32, a.shape, 1)
    k = 1
    while k < SUBLANES:
        b = b + a * jnp.where(sub >= k, pltpu.roll(b, k, axis=1), 0.0)
        a = a * jnp.where(sub >= k, pltpu.roll(a, k, axis=1), 1.0)
        k *= 2
    carry = h_ref[0:1, :]
    rows = []
    for gi in range(groups):
        rows.append(a[gi] * carry + b[gi])
        carry = rows[-1][SUBLANES - 1:SUBLANES, :]
    hs = jnp.concatenate(rows, axis=0)
    h_ref[...] = jnp.broadcast_to(carry, h_ref.shape)

    mix_ref[...] = x + _dot((hs * y).astype(BF16), wo_ref[...])


def _rg_block(x, g, w_in, conv_w, conv_b, w_a, b_a, w_x, b_x, lam, w_out, g_mlp, w_up, w_down):
    bsz, s, d = x.shape
    n = s // SCAN_TILE
    weights = (g, w_in, conv_w, conv_b, w_a, b_a, w_x, b_x, lam, w_out, g_mlp, w_up, w_down)
    return pl.pallas_call(
        _rg_layer_kernel,
        grid=(bsz, n + 1),
        in_specs=[pl.BlockSpec((None, SCAN_TILE, d), lambda b, i: (b, jnp.minimum(i, n - 1), 0))]
        + [_resident(w.shape) for w in weights],
        out_specs=pl.BlockSpec((None, SCAN_TILE, d), lambda b, i: (b, jnp.maximum(i - 1, 0), 0)),
        out_shape=jax.ShapeDtypeStruct((bsz, s, d), F32),
        scratch_shapes=[pltpu.VMEM((8, d), F32), pltpu.VMEM((8, d), F32),
                        pltpu.VMEM((SCAN_TILE, d), F32)],
        compiler_params=_cparams("parallel", "arbitrary"),
        name="rg_block",
    )(x, *weights)


def _pos_terms(pos):
    lane = lax.broadcasted_iota(jnp.int32, (pos.shape[0], AUX_COLS), 1)
    hi = ((pos // SLC_LEN) * SLC_LEN).astype(F32)
    lo = (pos % SLC_LEN).astype(F32)
    return jnp.where(lane < 4, jnp.where(lane % 2 == 0, hi, lo), 0.0).astype(BF16)


def _nsa_in_kernel(x_ref, g_ref, wq_ref, wc_ref, wk_ref, wv_ref, wg_ref, bg_ref,
                   qt_ref, kvc_ref, ks_ref, kw_ref, vst_ref, vwt_ref, gt_ref):
    tm = x_ref.shape[0]
    dh = HEAD_DIM
    ng = ks_ref.shape[0]
    h = _rmsnorm(x_ref[...], g_ref[...]).astype(BF16)
    qt_ref[...] = (_dot_nt(wq_ref[...], h) * (dh ** -0.5 * LOG2E)).astype(qt_ref.dtype)
    kvc_ref[...] = _dot(h, wc_ref[...])
    gt_ref[...] = jax.nn.sigmoid(_dot_nt(wg_ref[...], h) + bg_ref[...])

    pos = pl.program_id(1) * tm + lax.broadcasted_iota(jnp.int32, (tm, 1), 0)
    terms = _pos_terms(pos)
    blk_lane = lax.broadcasted_iota(jnp.int32, (tm, MAX_BLOCKS), 1)
    onehot = jnp.where(blk_lane == pos // SLC_LEN, 1.0, 0.0).astype(BF16)
    k = _dot(h, wk_ref[...]).astype(BF16)
    for gi in range(ng):
        ks_pos = jnp.concatenate([k[:, gi * dh:(gi + 1) * dh], terms], axis=1)
        ks_ref[gi] = jnp.concatenate([onehot, ks_pos], axis=1)
        kw_ref[gi] = jnp.concatenate([k[:, (ng + gi) * dh:(ng + gi + 1) * dh], terms], axis=1)
    vt = _dot_nt(wv_ref[...], h).astype(BF16)
    pad_row = lax.broadcasted_iota(jnp.int32, (V_PAD, KEY_TILE), 0)
    ones_rows = jnp.where(pad_row == 0, 1.0, 0.0).astype(BF16)
    for gi in range(ng):
        for c in range(tm // KEY_TILE):
            cols = slice(c * KEY_TILE, (c + 1) * KEY_TILE)
            vst_ref[gi, c] = jnp.concatenate([vt[gi * dh:(gi + 1) * dh, cols], ones_rows], axis=0)
            vwt_ref[gi, c] = jnp.concatenate(
                [vt[(ng + gi) * dh:(ng + gi + 1) * dh, cols], ones_rows], axis=0)


def _nsa_in(x, g, wq_t, wc, wk, wv_t, wg_t, bg_t):
    bsz, s, d = x.shape
    ng, dh, kt = N_KV_GROUPS, HEAD_DIM, KEY_TILE
    per = ROW_TILE // kt
    return pl.pallas_call(
        _nsa_in_kernel,
        grid=(bsz, s // ROW_TILE),
        in_specs=[pl.BlockSpec((None, ROW_TILE, d), lambda b, i: (b, i, 0)),
                  _resident(g.shape), _resident(wq_t.shape), _resident(wc.shape),
                  _resident(wk.shape), _resident(wv_t.shape), _resident(wg_t.shape),
                  _resident(bg_t.shape)],
        out_specs=[
            pl.BlockSpec((None, wq_t.shape[0], ROW_TILE), lambda b, i: (b, 0, i)),
            pl.BlockSpec((None, ROW_TILE, wc.shape[1]), lambda b, i: (b, i, 0)),
            pl.BlockSpec((None, ng, ROW_TILE, dh + MAX_BLOCKS + AUX_COLS), lambda b, i: (b, 0, i, 0)),
            pl.BlockSpec((None, ng, ROW_TILE, dh + AUX_COLS), lambda b, i: (b, 0, i, 0)),
            pl.BlockSpec((None, ng, per, dh + V_PAD, kt), lambda b, i: (b, 0, i, 0, 0)),
            pl.BlockSpec((None, ng, per, dh + V_PAD, kt), lambda b, i: (b, 0, i, 0, 0)),
            pl.BlockSpec((None, wg_t.shape[0], ROW_TILE), lambda b, i: (b, 0, i)),
        ],
        out_shape=[
            jax.ShapeDtypeStruct((bsz, wq_t.shape[0], s), BF16),
            jax.ShapeDtypeStruct((bsz, s, wc.shape[1]), F32),
            jax.ShapeDtypeStruct((bsz, ng, s, dh + MAX_BLOCKS + AUX_COLS), BF16),
            jax.ShapeDtypeStruct((bsz, ng, s, dh + AUX_COLS), BF16),
            jax.ShapeDtypeStruct((bsz, ng, s // kt, dh + V_PAD, kt), BF16),
            jax.ShapeDtypeStruct((bsz, ng, s // kt, dh + V_PAD, kt), BF16),
            jax.ShapeDtypeStruct((bsz, wg_t.shape[0], s), F32),
        ],
        compiler_params=_cparams("parallel", "parallel"),
        name="nsa_in",
    )(x, g, wq_t, wc, wk, wv_t, wg_t, bg_t)


def _compress_kernel(zk_ref, zv_ref, pe_ref, w1_ref, w2_ref, kc_ref, vct_ref):
    nchunk = kc_ref.shape[1]
    dh = HEAD_DIM

    def pre_act(z_ref, t):
        acc = [jnp.zeros((nchunk, LANES), F32) for _ in range(CMP_LEN // CMP_STRIDE)]
        for p in range(CMP_STRIDE):
            zp = z_ref[pl.ds(p, nchunk, stride=CMP_STRIDE), :]
            for r in range(len(acc)):
                pp = r * CMP_STRIDE + p
                acc[r] = acc[r] + _dot((zp + pe_ref[t, pp:pp + 1, :]).astype(BF16), w1_ref[t, pp])
        return _gelu_tanh(acc[0] + pltpu.roll(acc[1], nchunk - 1, axis=0)).astype(BF16)

    kc = _dot(pre_act(zk_ref, 0), w2_ref[0]).astype(BF16)
    cmp_end = lax.broadcasted_iota(jnp.int32, (nchunk, 1), 0) * CMP_STRIDE + (CMP_LEN - 1)
    terms = _pos_terms(cmp_end)
    for gi in range(kc_ref.shape[0]):
        kc_ref[gi] = jnp.concatenate([kc[:, gi * dh:(gi + 1) * dh], terms], axis=1)
    vct = _dot_nt(w2_ref[1], pre_act(zv_ref, 1)).astype(BF16)
    for gi in range(vct_ref.shape[0]):
        vct_ref[gi] = vct[gi * dh:(gi + 1) * dh, :]


def _compress(kvc, pe, w1, w2):
    bsz, s, _ = kvc.shape
    nchunk = s // CMP_STRIDE
    halves = N_KV_GROUPS // CMP_GROUPS
    return pl.pallas_call(
        _compress_kernel,
        grid=(bsz, halves),
        in_specs=[
            pl.BlockSpec((None, s, LANES), lambda b, c: (b, 0, c)),
            pl.BlockSpec((None, s, LANES), lambda b, c: (b, 0, halves + c)),
            _resident(pe.shape), _resident(w1.shape), _resident(w2.shape),
        ],
        out_specs=[
            pl.BlockSpec((None, CMP_GROUPS, nchunk, HEAD_DIM + AUX_COLS), lambda b, c: (b, c, 0, 0)),
            pl.BlockSpec((None, CMP_GROUPS, HEAD_DIM, nchunk), lambda b, c: (b, c, 0, 0)),
        ],
        out_shape=[
            jax.ShapeDtypeStruct((bsz, N_KV_GROUPS, nchunk, HEAD_DIM + AUX_COLS), BF16),
            jax.ShapeDtypeStruct((bsz, N_KV_GROUPS, HEAD_DIM, nchunk), BF16),
        ],
        compiler_params=_cparams("parallel", "parallel"),
        name="nsa_compress",
    )(kvc, kvc, pe, w1, w2)


def _split3(x):
    hi = x.astype(BF16)
    r = x - hi.astype(F32)
    mid = r.astype(BF16)
    lo = (r - mid.astype(F32)).astype(BF16)
    return hi, mid, lo


def _key_tile(k_ref, p):
    return k_ref[pl.ds(pl.multiple_of(p * KEY_TILE, KEY_TILE), KEY_TILE), :]


def _col_max(scores, init=None):
    parts = [jnp.max(s, axis=0, keepdims=True) for s in scores]
    return functools.reduce(jnp.maximum, parts if init is None else [init] + parts)


def _scores(k_ref, tiles, q_aug, masks):
    out = []
    for p, mask in zip(tiles, masks):
        s = _dot(_key_tile(k_ref, p), q_aug)
        if mask is None:
            out.append(s)
        elif mask.dtype == jnp.bool_:
            out.append(jnp.where(mask, s, NEG_INF))
        else:
            out.append(s + mask)
    return out


def _weighted_values(vt_ref, tiles, scores, m):
    acc = None
    for i in range(0, len(tiles), 2):
        vt = jnp.concatenate([vt_ref[p] for p in tiles[i:i + 2]], axis=1)
        e = jnp.concatenate([jnp.exp2((s - m).astype(BF16)) for s in scores[i:i + 2]], axis=0)
        pv = _dot(vt, e)
        acc = pv if acc is None else acc + pv
    return acc


def _normalized(acc):
    dh = HEAD_DIM
    return acc[:dh] * (1.0 / jnp.maximum(acc[dh:dh + 1], 1e-30))


def _attn_kernel(qt_ref, gt_ref, kc_ref, vct_ref, ks_ref, vst_ref, kw_ref, vwt_ref, cmap_ref,
                 slope_ref, o_ref, list_ref, ms_ref, as_ref, sbuf_ref, *, n_sel):
    hg, dh, tq, tk = HEADS_PER_GROUP, HEAD_DIM, Q_TILE, KEY_TILE
    ncol = hg * tq
    nkt = vst_ref.shape[0] - 1
    qb = pl.program_id(2)
    t0 = qb * tq
    heads = range(hg)
    t_col = t0 + lax.broadcasted_iota(jnp.int32, (1, tq), 1)
    qt = qt_ref[...]
    slope_rows = slope_ref[...]
    q_pos = [jnp.concatenate([qt[h * dh:(h + 1) * dh, :], slope_rows[:, h * tq:(h + 1) * tq]], axis=0)
             for h in heads]

    nb = cmap_ref.shape[0]
    cur = t_col // SLC_LEN

    def compress_and_select(level):
        nb_v = LEVEL_BLOCKS * (level + 1)
        nc_v = nb_v * SLC_LEN // CMP_STRIDE
        cmp_end = lax.broadcasted_iota(jnp.int32, (nc_v, 1), 0) * CMP_STRIDE + (CMP_LEN - 1)
        m_c = cmp_end <= t_col
        kc, vct = kc_ref[0:nc_v, :], vct_ref[:, 0:nc_v]
        o_c, p_sum = [], None
        s_c_all = [_dot(kc, q_pos[h]) for h in heads]
        for h in heads:
            s_c = jnp.where(m_c, s_c_all[h], NEG_INF)
            m = jnp.maximum(jnp.max(s_c, axis=0, keepdims=True), SOFTMAX_FLOOR)
            p_c = jnp.exp2(s_c - m)
            p_c = p_c * (1.0 / jnp.maximum(jnp.sum(p_c, axis=0, keepdims=True), 1e-30))
            o_c.append(_dot(vct, p_c.astype(BF16)))
            p_sum = p_c if p_sum is None else p_sum + p_c

        cmap = cmap_ref[0:nb_v, 0:nc_v]
        imp = sum(_dot(cmap, part) for part in _split3(p_sum))
        blk = lax.broadcasted_iota(jnp.int32, (nb_v, tq), 0)
        forced = (blk == 0) | (blk == cur) | (blk == cur - 1)
        visible = blk <= cur
        score = jnp.where(visible, jnp.where(forced, -jnp.inf, imp), -1.0)
        blk_f = blk.astype(F32)
        for _ in range(n_sel - N_FORCED):
            best = jnp.max(score, axis=0, keepdims=True)
            first = jnp.min(jnp.where(score == best, blk_f, float(nb)), axis=0, keepdims=True)
            score = jnp.where(blk_f == first, -jnp.inf, score)
        sel = jnp.where(visible & (score == -jnp.inf), 1.0, 0.0)
        if nb_v < nb:
            sel = jnp.concatenate([sel, jnp.zeros((nb - nb_v, tq), F32)], axis=0)
        return jnp.concatenate(o_c, axis=0), sel

    n_levels = -(-(nkt * tk // SLC_LEN) // LEVEL_BLOCKS)
    level = (t0 + tq - 1) // (LEVEL_BLOCKS * SLC_LEN)
    o_c, sel = lax.switch(level, [functools.partial(compress_and_select, v) for v in range(n_levels)])
    o_c = [o_c[h * dh:(h + 1) * dh] for h in heads]

    cnt = _dot_nt(jnp.ones((8, tq), BF16), sel.astype(BF16))
    blk_any = jnp.where(cnt > 0.0, 1.0, 0.0).astype(BF16)
    r_i = lax.broadcasted_iota(jnp.int32, (nb, LANES), 0)
    c_i = lax.broadcasted_iota(jnp.int32, (nb, LANES), 1)
    bit = jnp.left_shift(1, r_i % FLAG_BITS).astype(F32)
    words = _dot(blk_any, jnp.where(r_i // FLAG_BITS == c_i, bit, 0.0).astype(BF16))
    words = [words[0, w].astype(jnp.int32) for w in range(nb // FLAG_BITS)]

    per_tile = tk // SLC_LEN
    n_w, n_q = WINDOW // tk, tq // tk
    n_before = t0 // tk
    n = jnp.int32(0)
    for p in range(nkt - n_q):
        word = words[p * per_tile // FLAG_BITS]
        hit = (jnp.right_shift(word, p * per_tile % FLAG_BITS) & (2 ** per_tile - 1)) != 0
        list_ref[n] = jnp.int32(p)
        n = n + (hit & (p < n_before)).astype(jnp.int32)
    for j in range(SEL_CHUNK):
        list_ref[n + j] = jnp.int32(nkt)

    key_row = lax.broadcasted_iota(jnp.int32, (tk, 1), 0)
    w_tiles, w_masks = [], []
    for j in range(n_w + n_q):
        p = n_before - n_w + j
        exists = p >= 0
        p = jnp.maximum(p, 0)
        w_tiles.append(p)
        if j < n_q:
            pos = jnp.where(exists, p * tk, -FAR_POSITION) + key_row
            w_masks.append(t_col - pos < WINDOW)
        elif j < n_w:
            w_masks.append(jnp.where(exists, 0.0, NEG_INF))
        else:
            w_masks.append(p * tk + key_row <= t_col)
    s_w = [_scores(kw_ref, w_tiles, q_pos[h], w_masks) for h in heads]

    bias = jnp.where(sel > 0.0, 0.0, -MASK_BIAS).astype(BF16)
    q_sel = [jnp.concatenate([bias, q_pos[h]], axis=0) for h in heads]
    d_tiles = [n_before + j for j in range(n_q)]
    d_masks = [p * tk + key_row <= t_col for p in d_tiles]
    s_d = [_scores(ks_ref, d_tiles, q_sel[h], d_masks) for h in heads]

    def chunk_tiles(c):
        return [list_ref[c * SEL_CHUNK + j] for j in range(SEL_CHUNK)]

    def chunk_scores(c, slot):
        tiles = chunk_tiles(c)
        for h in heads:
            for j, s in enumerate(_scores(ks_ref, tiles, q_sel[h], [None] * SEL_CHUNK)):
                sbuf_ref[slot, h, j] = s

    def chunk_update(c, slot):
        tiles = chunk_tiles(c)
        for h in heads:
            cols = slice(h * tq, (h + 1) * tq)
            s_s = [sbuf_ref[slot, h, j] for j in range(SEL_CHUNK)]
            m_prev = ms_ref[:, cols]
            m_new = _col_max(s_s, m_prev)
            ms_ref[:, cols] = m_new
            as_ref[:, cols] = (jnp.exp2(m_prev - m_new) * as_ref[:, cols]
                               + _weighted_values(vst_ref, tiles, s_s, m_new))

    chunk_scores(0, 0)
    o_w = [_normalized(_weighted_values(vwt_ref, w_tiles, s_w[h], _col_max(s_w[h]))) for h in heads]
    for h in heads:
        cols = slice(h * tq, (h + 1) * tq)
        m_d = _col_max(s_d[h])
        ms_ref[:, cols] = m_d
        as_ref[:, cols] = _weighted_values(vst_ref, d_tiles, s_d[h], m_d)

    def two_chunks(i, carry):
        chunk_scores(2 * i + 1, 1)
        chunk_update(2 * i, 0)
        chunk_scores(2 * i + 2, 0)
        chunk_update(2 * i + 1, 1)
        return carry

    n_chunks = jnp.maximum((n + SEL_CHUNK - 1) // SEL_CHUNK, 1)
    lax.fori_loop(0, (n_chunks - 1) // 2, two_chunks, 0)

    @pl.when(n_chunks % 2 == 0)
    def _():
        chunk_scores(n_chunks - 1, 1)
        chunk_update(n_chunks - 2, 0)
        chunk_update(n_chunks - 1, 1)

    @pl.when(n_chunks % 2 == 1)
    def _():
        chunk_update(n_chunks - 1, 0)

    gt = gt_ref[...]
    out = []
    for h in heads:
        r = 3 * h
        o_s = _normalized(as_ref[:, h * tq:(h + 1) * tq])
        o_h = gt[r:r + 1] * o_c[h] + gt[r + 1:r + 2] * o_s + gt[r + 2:r + 3] * o_w[h]
        out.append(o_h.T)
    o_ref[...] = jnp.concatenate(out, axis=1).astype(o_ref.dtype)


def _attend(qt, gates_t, kc, vct, ks, vst, kw, vwt, cmap, slopes):
    bsz, _, s = qt.shape
    hg, dh, tq = HEADS_PER_GROUP, HEAD_DIM, Q_TILE
    nchunk = kc.shape[2]
    n_sel = min(N_SELECT, s // SLC_LEN)
    nkt = s // KEY_TILE
    assert tq % KEY_TILE == 0 and tq <= WINDOW and n_sel >= N_FORCED
    per_bg = lambda *tail: pl.BlockSpec((None, None) + tail, lambda b, g, i: (b, g) + (0,) * len(tail))
    null_k = jnp.zeros((KEY_TILE, ks.shape[-1]), BF16).at[:, MAX_BLOCKS + dh + NULL_COL].set(1.0)
    ks = jnp.concatenate([ks, jnp.broadcast_to(null_k, ks.shape[:2] + null_k.shape)], axis=2)
    vst = jnp.concatenate([vst, jnp.zeros(vst.shape[:2] + (1,) + vst.shape[3:], BF16)], axis=2)
    return pl.pallas_call(
        functools.partial(_attn_kernel, n_sel=n_sel),
        grid=(bsz, N_KV_GROUPS, s // tq),
        in_specs=[
            pl.BlockSpec((None, hg * dh, tq), lambda b, g, i: (b, g, i)),
            pl.BlockSpec((None, GATE_ROWS, tq), lambda b, g, i: (b, g, i)),
            per_bg(nchunk, dh + AUX_COLS), per_bg(dh, nchunk),
            per_bg(*ks.shape[2:]), per_bg(*vst.shape[2:]),
            per_bg(*kw.shape[2:]), per_bg(*vwt.shape[2:]),
            pl.BlockSpec(cmap.shape, lambda b, g, i: (0, 0)),
            pl.BlockSpec((None, AUX_COLS, hg * tq), lambda b, g, i: (g, 0, 0)),
        ],
        out_specs=pl.BlockSpec((None, tq, hg * dh), lambda b, g, i: (b, i, g)),
        out_shape=jax.ShapeDtypeStruct((bsz, s, N_HEADS * dh), BF16),
        scratch_shapes=[pltpu.SMEM((nkt + SEL_CHUNK,), jnp.int32),
                        pltpu.VMEM((1, hg * tq), F32), pltpu.VMEM((dh + V_PAD, hg * tq), F32),
                        pltpu.VMEM((2, hg, SEL_CHUNK, KEY_TILE, tq), F32)],
        compiler_params=_cparams("parallel", "parallel", "arbitrary"),
        name="nsa_attend",
    )(qt, gates_t, kc, vct, ks, vst, kw, vwt, cmap, slopes)


def _nsa_out_kernel(o_ref, wo_ref, x_ref, gm_ref, wu_ref, wd_ref, gf_ref, y_ref, *, final_norm):
    y = _mlp_rows(x_ref[...] + _dot(o_ref[...], wo_ref[...]), gm_ref, wu_ref, wd_ref)
    y_ref[...] = _rmsnorm(y, gf_ref[...]) if final_norm else y


def _nsa_out(o2, w_out, x2, g_mlp, w_up, w_down, g_final, final_norm):
    n, d = x2.shape
    weights = (g_mlp, w_up, w_down, g_final)
    return pl.pallas_call(
        functools.partial(_nsa_out_kernel, final_norm=final_norm),
        grid=(n // ROW_TILE,),
        in_specs=[pl.BlockSpec((ROW_TILE, o2.shape[1]), lambda i: (i, 0)), _resident(w_out.shape),
                  pl.BlockSpec((ROW_TILE, d), lambda i: (i, 0))]
        + [_resident(w.shape) for w in weights],
        out_specs=pl.BlockSpec((ROW_TILE, d), lambda i: (i, 0)),
        out_shape=jax.ShapeDtypeStruct((n, d), F32),
        compiler_params=_cparams("parallel"),
        name="nsa_out_mlp",
    )(o2, w_out, x2, *weights)


def _block_diag(w):
    eye = jnp.eye(CMP_GROUPS, dtype=w.dtype)
    out = jnp.einsum("gh,...ij->...gihj", eye, w)
    return out.reshape(*w.shape[:-2], CMP_GROUPS * w.shape[-2], CMP_GROUPS * w.shape[-1])


def _cmp_to_slc_map(nchunk, nb):
    cs = jnp.arange(nchunk) * CMP_STRIDE
    ss = jnp.arange(MAX_BLOCKS) * SLC_LEN
    ov = jnp.minimum(cs[None, :] + CMP_LEN, ss[:, None] + SLC_LEN) - jnp.maximum(cs[None, :], ss[:, None])
    ov = jnp.where(jnp.arange(MAX_BLOCKS)[:, None] < nb, ov, 0)
    return (jnp.maximum(ov, 0) / CMP_STRIDE).astype(BF16)


def _alibi_slope_rows():
    h = jnp.arange(1, N_HEADS + 1, dtype=F32)
    slope = jnp.exp2(-ALIBI_MAX * h / N_HEADS) * LOG2E
    hi = slope.astype(BF16)
    lo = (slope - hi.astype(F32)).astype(BF16)
    null = jnp.full_like(hi, -MASK_BIAS)
    rows = jnp.stack([hi, hi, lo, lo, null], axis=0)
    rows = jnp.pad(rows, ((0, AUX_COLS - rows.shape[0]), (0, 0)))
    rows = rows.reshape(AUX_COLS, N_KV_GROUPS, HEADS_PER_GROUP).transpose(1, 0, 2)
    return jnp.repeat(rows, Q_TILE, axis=2)


def _rg_layer(x, g, w_in, conv_w, conv_b, w_a, b_a, w_x, b_x, lam, w_out, g_mlp, w_up, w_down):
    row = lambda v: v.reshape(1, -1)
    return _rg_block(x, row(g), w_in.astype(BF16), conv_w, row(conv_b), w_a.astype(BF16), row(b_a),
                     w_x.astype(BF16), row(b_x), row(lam), w_out.astype(BF16),
                     row(g_mlp), w_up.astype(BF16), w_down.astype(BF16))


def _nsa_layer(x, g, w_in, b_gate, pe_k, pe_v, w1_k, w2_k, w1_v, w2_v, w_out,
               g_mlp, w_up, w_down, g_final, final_norm):
    bsz, s, d = x.shape
    assert s // SLC_LEN <= MAX_BLOCKS and s % ROW_TILE == 0
    q_cols = N_HEADS * HEAD_DIM
    kv_cols = N_KV_GROUPS * HEAD_DIM
    kv = w_in[:, q_cols:q_cols + 6 * kv_cols].reshape(d, 6, kv_cols)
    wq_t = w_in[:, :q_cols].T.astype(BF16)
    wc = kv[:, 0:2].reshape(d, 2 * kv_cols).astype(BF16)
    wk = jnp.concatenate([kv[:, 2], kv[:, 4]], axis=1).astype(BF16)
    wv_t = jnp.concatenate([kv[:, 3], kv[:, 5]], axis=1).T.astype(BF16)
    per_group = 3 * HEADS_PER_GROUP
    wg = w_in[:, q_cols + 6 * kv_cols:].reshape(d, N_KV_GROUPS, per_group)
    wg = jnp.pad(wg, ((0, 0), (0, 0), (0, GATE_ROWS - per_group))).reshape(d, -1)
    bg = jnp.pad(b_gate.reshape(N_KV_GROUPS, per_group), ((0, 0), (0, GATE_ROWS - per_group)))
    qt, kvc, ks, kw, vst, vwt, gates_t = _nsa_in(
        x, g.reshape(1, d), wq_t, wc, wk, wv_t, wg.T.astype(BF16), bg.reshape(-1, 1))

    pe = jnp.stack([jnp.tile(pe_k, (1, CMP_GROUPS)), jnp.tile(pe_v, (1, CMP_GROUPS))])
    w1 = jnp.stack([w1_k, w1_v]).reshape(2, CMP_LEN, HEAD_DIM, HEAD_DIM)
    w2 = jnp.stack([_block_diag(w2_k), _block_diag(w2_v).T])
    kc, vct = _compress(kvc, pe, _block_diag(w1).astype(BF16), w2.astype(BF16))

    cmap = _cmp_to_slc_map(s // CMP_STRIDE, s // SLC_LEN)
    o = _attend(qt, gates_t, kc, vct, ks, vst, kw, vwt, cmap, _alibi_slope_rows())
    y = _nsa_out(o.reshape(bsz * s, -1), w_out.astype(BF16), x.reshape(bsz * s, d),
                 g_mlp.reshape(1, d), w_up.astype(BF16), w_down.astype(BF16),
                 g_final.reshape(1, d), final_norm)
    return y.reshape(bsz, s, d)


def kernel(x, norm_mix, norm_ffn, norm_final, rg_w_in, rg_conv_w, rg_conv_b, rg_w_a, rg_b_a, rg_w_x, rg_b_x, rg_lambda, rg_w_out, nsa_w_in, nsa_b_gate, nsa_pe_k, nsa_pe_v, nsa_w1_k, nsa_w2_k, nsa_w1_v, nsa_w2_v, nsa_w_out, mlp_w_up, mlp_w_down):
    depth = norm_mix.shape[0]
    assert depth % 2 == 0
    for i in range(depth):
        j = i // 2
        if i % 2 == 0:
            x = _rg_layer(x, norm_mix[i], rg_w_in[j], rg_conv_w[j], rg_conv_b[j], rg_w_a[j],
                          rg_b_a[j], rg_w_x[j], rg_b_x[j], rg_lambda[j], rg_w_out[j],
                          norm_ffn[i], mlp_w_up[i], mlp_w_down[i])
        else:
            x = _nsa_layer(x, norm_mix[i], nsa_w_in[j], nsa_b_gate[j], nsa_pe_k[j], nsa_pe_v[j],
                           nsa_w1_k[j], nsa_w2_k[j], nsa_w1_v[j], nsa_w2_v[j], nsa_w_out[j],
                           norm_ffn[i], mlp_w_up[i], mlp_w_down[i], norm_final,
                           final_norm=(i == depth - 1))
    return x
```

```python
import functools
import math

import jax
import jax.numpy as jnp
from jax import lax
from jax.experimental import pallas as pl
from jax.experimental.pallas import tpu as pltpu

F32 = jnp.float32
BF16 = jnp.bfloat16

RMS_EPS = 1e-6
RG_BLOCKS = 4
CONV_W = 4
RG_C = 8.0
N_HEADS = 16
HEAD_DIM = 64
N_KV_GROUPS = 4
HEADS_PER_GROUP = N_HEADS // N_KV_GROUPS
CMP_LEN = 32
CMP_STRIDE = 16
SLC_LEN = 64
N_SELECT = 16
WINDOW = 512
ALIBI_MAX = 8.0
NEG_INF = -1e30
FORCE_BONUS = 1e4

VMEM_LIMIT_BYTES = 56 * 1024 * 1024
LANES = 128
SUBLANES = 8
CMP_GROUPS = LANES // HEAD_DIM

ROW_TILE = 512
SCAN_TILE = 256
Q_TILE = 256
KEY_TILE = 128
FF_CHUNK = 1024

MAX_BLOCKS = 128
AUX_COLS = 64
GATE_ROWS = 16
MASK_BIAS = 2.0 ** 100
SOFTMAX_FLOOR = -1e25
LEVEL_BLOCKS = 32
NULL_COL = 4
V_PAD = 16
LOG2E = math.log2(math.e)
FLAG_BITS = 16
N_FORCED = 3
SEL_CHUNK = 4
FAR_POSITION = 2 ** 30


def _cparams(*sem):
    return pltpu.CompilerParams(dimension_semantics=sem, vmem_limit_bytes=VMEM_LIMIT_BYTES)


def _resident(shape):
    nd = len(shape)
    return pl.BlockSpec(shape, lambda *_: (0,) * nd, pipeline_mode=pl.Buffered(1))


def _rmsnorm(x, g):
    y = x * lax.rsqrt(jnp.mean(x * x, axis=-1, keepdims=True) + RMS_EPS)
    return y * g


def _gelu_tanh(x):
    return jax.nn.gelu(x, approximate=True)


def _dot(a, b):
    return jnp.dot(a, b, preferred_element_type=F32)


def _dot_nt(a, b):
    return lax.dot_general(a, b, (((1,), (1,)), ((), ())), preferred_element_type=F32)


def _sigmoid(x):
    return 0.5 * jnp.tanh(0.5 * x) + 0.5


def _mlp_rows(x, g_ref, wu_ref, wd_ref):
    h = _rmsnorm(x, g_ref[...]).astype(BF16)
    acc = x
    for c in range(wu_ref.shape[1] // FF_CHUNK):
        cols = slice(c * FF_CHUNK, (c + 1) * FF_CHUNK)
        u = jnp.maximum(_dot(h, wu_ref[:, cols]), 0.0)
        acc = acc + _dot((u * u).astype(BF16), wd_ref[cols, :])
    return acc


def _rg_layer_kernel(x_ref, g_ref, win_ref, cw_ref, cb_ref, wa_ref, ba_ref, wx_ref, bx_ref,
                     lam_ref, wo_ref, gm_ref, wu_ref, wd_ref, o_ref, tail_ref, h_ref, mix_ref):
    ts, d = x_ref.shape
    bw = d // RG_BLOCKS

    @pl.when(pl.program_id(1) == 0)
    def _():
        tail_ref[...] = jnp.zeros_like(tail_ref)
        h_ref[...] = jnp.zeros_like(h_ref)
        mix_ref[...] = jnp.zeros_like(mix_ref)

    x = x_ref[...]
    proj = _dot(_rmsnorm(x, g_ref[...]).astype(BF16), win_ref[...])
    y = _gelu_tanh(proj[:, :d])
    xp = proj[:, d:]
    ext = jnp.concatenate([tail_ref[...], xp], axis=0)
    cw = cw_ref[...]
    xb = cb_ref[...] + cw[CONV_W - 1:CONV_W] * xp
    for k in range(1, CONV_W):
        xb = xb + cw[CONV_W - 1 - k:CONV_W - k] * pltpu.roll(ext, k, axis=0)[8:]
    tail_ref[...] = xp[ts - 8:]

    xb16 = xb.astype(BF16)
    ra = jnp.concatenate(
        [_dot(xb16[:, n * bw:(n + 1) * bw], wa_ref[n]) for n in range(RG_BLOCKS)], axis=1)
    rx = jnp.concatenate(
        [_dot(xb16[:, n * bw:(n + 1) * bw], wx_ref[n]) for n in range(RG_BLOCKS)], axis=1)

    o_ref[...] = _mlp_rows(mix_ref[...], gm_ref, wu_ref, wd_ref)

    r = _sigmoid(ra + ba_ref[...])
    i = _sigmoid(rx + bx_ref[...])
    nl = -lam_ref[...]
    softplus = jnp.maximum(nl, 0.0) + jnp.log1p(jnp.exp(-jnp.abs(nl)))
    log_a = -RG_C * r * softplus
    a = jnp.exp(log_a)
    b = jnp.sqrt(-jnp.tanh(log_a) * (a * a + 1.0)) * (i * xb)

    groups = ts // SUBLANES
    a = a.reshape(groups, SUBLANES, d)
    b = b.reshape(groups, SUBLANES, d)
    sub = lax.broadcasted_iota(jnp.int32, a.shape, 1)
    k = 1
    while k < SUBLANES:
        b = b + a * jnp.where(sub >= k, pltpu.roll(b, k, axis=1), 0.0)
        a = a * jnp.where(sub >= k, pltpu.roll(a, k, axis=1), 1.0)
        k *= 2
    carry = h_ref[0:1, :]
    rows = []
    for gi in range(groups):
        rows.append(a[gi] * carry + b[gi])
        carry = rows[-1][SUBLANES - 1:SUBLANES, :]
    hs = jnp.concatenate(rows, axis=0)
    h_ref[...] = jnp.broadcast_to(carry, h_ref.shape)

    mix_ref[...] = x + _dot((hs * y).astype(BF16), wo_ref[...])


def _rg_block(x, g, w_in, conv_w, conv_b, w_a, b_a, w_x, b_x, lam, w_out, g_mlp, w_up, w_down):
    bsz, s, d = x.shape
    n = s // SCAN_TILE
    weights = (g, w_in, conv_w, conv_b, w_a, b_a, w_x, b_x, lam, w_out, g_mlp, w_up, w_down)
    return pl.pallas_call(
        _rg_layer_kernel,
        grid=(bsz, n + 1),
        in_specs=[pl.BlockSpec((None, SCAN_TILE, d), lambda b, i: (b, jnp.minimum(i, n - 1), 0))]
        + [_resident(w.shape) for w in weights],
        out_specs=pl.BlockSpec((None, SCAN_TILE, d), lambda b, i: (b, jnp.maximum(i - 1, 0), 0)),
        out_shape=jax.ShapeDtypeStruct((bsz, s, d), F32),
        scratch_shapes=[pltpu.VMEM((8, d), F32), pltpu.VMEM((8, d), F32),
                        pltpu.VMEM((SCAN_TILE, d), F32)],
        compiler_params=_cparams("parallel", "arbitrary"),
        name="rg_block",
    )(x, *weights)


def _pos_terms(pos):
    lane = lax.broadcasted_iota(jnp.int32, (pos.shape[0], AUX_COLS), 1)
    hi = ((pos // SLC_LEN) * SLC_LEN).astype(F32)
    lo = (pos % SLC_LEN).astype(F32)
    return jnp.where(lane < 4, jnp.where(lane % 2 == 0, hi, lo), 0.0).astype(BF16)


def _nsa_in_kernel(x_ref, g_ref, wq_ref, wc_ref, wk_ref, wv_ref, wg_ref, bg_ref,
                   qt_ref, kvc_ref, ks_ref, kw_ref, vst_ref, vwt_ref, gt_ref):
    tm = x_ref.shape[0]
    dh = HEAD_DIM
    ng = ks_ref.shape[0]
    is_pad = pl.program_id(1) == pl.num_programs(1) - 1
    h = _rmsnorm(x_ref[...], g_ref[...]).astype(BF16)
    qt_ref[...] = (_dot_nt(wq_ref[...], h) * (dh ** -0.5 * LOG2E)).astype(qt_ref.dtype)
    kvc_ref[...] = _dot(h, wc_ref[...])
    gt_ref[...] = _sigmoid(_dot_nt(wg_ref[...], h) + bg_ref[...])

    row = jnp.minimum(pl.program_id(1), pl.num_programs(1) - 2) * tm
    pos = row + lax.broadcasted_iota(jnp.int32, (tm, 1), 0)
    terms = _pos_terms(pos)
    blk_lane = lax.broadcasted_iota(jnp.int32, (tm, MAX_BLOCKS), 1)
    onehot = jnp.where(blk_lane == pos // SLC_LEN, 1.0, 0.0).astype(BF16)
    null_lane = lax.broadcasted_iota(jnp.int32, (tm, MAX_BLOCKS + dh + AUX_COLS), 1)
    null_keys = jnp.where(null_lane == MAX_BLOCKS + dh + NULL_COL, 1.0, 0.0).astype(BF16)
    k = _dot(h, wk_ref[...]).astype(BF16)
    for gi in range(ng):
        ks_pos = jnp.concatenate([k[:, gi * dh:(gi + 1) * dh], terms], axis=1)
        ks_ref[gi] = jnp.where(is_pad, null_keys, jnp.concatenate([onehot, ks_pos], axis=1))
        kw_ref[gi] = jnp.concatenate([k[:, (ng + gi) * dh:(ng + gi + 1) * dh], terms], axis=1)
    vt = _dot_nt(wv_ref[...], h).astype(BF16)
    pad_row = lax.broadcasted_iota(jnp.int32, (V_PAD, KEY_TILE), 0)
    ones_rows = jnp.where(pad_row == 0, 1.0, 0.0).astype(BF16)
    for gi in range(ng):
        for c in range(tm // KEY_TILE):
            cols = slice(c * KEY_TILE, (c + 1) * KEY_TILE)
            vs_tile = jnp.concatenate([vt[gi * dh:(gi + 1) * dh, cols], ones_rows], axis=0)
            vst_ref[gi, c] = jnp.where(is_pad, jnp.zeros_like(vs_tile), vs_tile)
            vwt_ref[gi, c] = jnp.concatenate(
                [vt[(ng + gi) * dh:(ng + gi + 1) * dh, cols], ones_rows], axis=0)


def _nsa_in(x, g, wq_t, wc, wk, wv_t, wg_t, bg_t):
    bsz, s, d = x.shape
    ng, dh, kt = N_KV_GROUPS, HEAD_DIM, KEY_TILE
    per = ROW_TILE // kt
    n = s // ROW_TILE
    last = lambda i: jnp.minimum(i, n - 1)
    return pl.pallas_call(
        _nsa_in_kernel,
        grid=(bsz, n + 1),
        in_specs=[pl.BlockSpec((None, ROW_TILE, d), lambda b, i: (b, last(i), 0)),
                  _resident(g.shape), _resident(wq_t.shape), _resident(wc.shape),
                  _resident(wk.shape), _resident(wv_t.shape), _resident(wg_t.shape),
                  _resident(bg_t.shape)],
        out_specs=[
            pl.BlockSpec((None, wq_t.shape[0], ROW_TILE), lambda b, i: (b, 0, last(i))),
            pl.BlockSpec((None, ROW_TILE, wc.shape[1]), lambda b, i: (b, last(i), 0)),
            pl.BlockSpec((None, ng, ROW_TILE, dh + MAX_BLOCKS + AUX_COLS), lambda b, i: (b, 0, i, 0)),
            pl.BlockSpec((None, ng, ROW_TILE, dh + AUX_COLS), lambda b, i: (b, 0, last(i), 0)),
            pl.BlockSpec((None, ng, per, dh + V_PAD, kt), lambda b, i: (b, 0, i, 0, 0)),
            pl.BlockSpec((None, ng, per, dh + V_PAD, kt), lambda b, i: (b, 0, last(i), 0, 0)),
            pl.BlockSpec((None, wg_t.shape[0], ROW_TILE), lambda b, i: (b, 0, last(i))),
        ],
        out_shape=[
            jax.ShapeDtypeStruct((bsz, wq_t.shape[0], s), BF16),
            jax.ShapeDtypeStruct((bsz, s, wc.shape[1]), F32),
            jax.ShapeDtypeStruct((bsz, ng, s + ROW_TILE, dh + MAX_BLOCKS + AUX_COLS), BF16),
            jax.ShapeDtypeStruct((bsz, ng, s, dh + AUX_COLS), BF16),
            jax.ShapeDtypeStruct((bsz, ng, s // kt + per, dh + V_PAD, kt), BF16),
            jax.ShapeDtypeStruct((bsz, ng, s // kt, dh + V_PAD, kt), BF16),
            jax.ShapeDtypeStruct((bsz, wg_t.shape[0], s), F32),
        ],
        compiler_params=_cparams("parallel", "arbitrary"),
        name="nsa_in",
    )(x, g, wq_t, wc, wk, wv_t, wg_t, bg_t)


def _compress_kernel(zk_ref, zv_ref, pe_ref, w1_ref, w2_ref, kc_ref, vct_ref):
    nchunk = kc_ref.shape[1]
    dh = HEAD_DIM

    def pre_act(z_ref, t):
        acc = [jnp.zeros((nchunk, LANES), F32) for _ in range(CMP_LEN // CMP_STRIDE)]
        for p in range(CMP_STRIDE):
            zp = z_ref[pl.ds(p, nchunk, stride=CMP_STRIDE), :]
            for r in range(len(acc)):
                pp = r * CMP_STRIDE + p
                acc[r] = acc[r] + _dot((zp + pe_ref[t, pp:pp + 1, :]).astype(BF16), w1_ref[t, pp])
        return _gelu_tanh(acc[0] + pltpu.roll(acc[1], nchunk - 1, axis=0)).astype(BF16)

    kc = _dot(pre_act(zk_ref, 0), w2_ref[0]).astype(BF16)
    cmp_end = lax.broadcasted_iota(jnp.int32, (nchunk, 1), 0) * CMP_STRIDE + (CMP_LEN - 1)
    terms = _pos_terms(cmp_end)
    for gi in range(kc_ref.shape[0]):
        kc_ref[gi] = jnp.concatenate([kc[:, gi * dh:(gi + 1) * dh], terms], axis=1)
    vct = _dot_nt(w2_ref[1], pre_act(zv_ref, 1)).astype(BF16)
    for gi in range(vct_ref.shape[0]):
        vct_ref[gi] = vct[gi * dh:(gi + 1) * dh, :]


def _compress(kvc, pe, w1, w2):
    bsz, s, _ = kvc.shape
    nchunk = s // CMP_STRIDE
    halves = N_KV_GROUPS // CMP_GROUPS
    return pl.pallas_call(
        _compress_kernel,
        grid=(bsz, halves),
        in_specs=[
            pl.BlockSpec((None, s, LANES), lambda b, c: (b, 0, c)),
            pl.BlockSpec((None, s, LANES), lambda b, c: (b, 0, halves + c)),
            _resident(pe.shape), _resident(w1.shape), _resident(w2.shape),
        ],
        out_specs=[
            pl.BlockSpec((None, CMP_GROUPS, nchunk, HEAD_DIM + AUX_COLS), lambda b, c: (b, c, 0, 0)),
            pl.BlockSpec((None, CMP_GROUPS, HEAD_DIM, nchunk), lambda b, c: (b, c, 0, 0)),
        ],
        out_shape=[
            jax.ShapeDtypeStruct((bsz, N_KV_GROUPS, nchunk, HEAD_DIM + AUX_COLS), BF16),
            jax.ShapeDtypeStruct((bsz, N_KV_GROUPS, HEAD_DIM, nchunk), BF16),
        ],
        compiler_params=_cparams("parallel", "parallel"),
        name="nsa_compress",
    )(kvc, kvc, pe, w1, w2)


def _split3(x):
    hi = x.astype(BF16)
    r = x - hi.astype(F32)
    mid = r.astype(BF16)
    lo = (r - mid.astype(F32)).astype(BF16)
    return hi, mid, lo


def _key_tile(k_ref, p):
    return k_ref[pl.ds(pl.multiple_of(p * KEY_TILE, KEY_TILE), KEY_TILE), :]


def _col_max(scores, init=None):
    parts = [jnp.max(s, axis=0, keepdims=True) for s in scores]
    return functools.reduce(jnp.maximum, parts if init is None else [init] + parts)


def _scores(k_ref, tiles, q_aug, masks):
    out = []
    for p, mask in zip(tiles, masks):
        s = _dot(_key_tile(k_ref, p), q_aug)
        if mask is None:
            out.append(s)
        elif mask.dtype == jnp.bool_:
            out.append(jnp.where(mask, s, NEG_INF))
        else:
            out.append(s + mask)
    return out


def _weighted_values(vt_ref, tiles, scores, m):
    acc = None
    for i in range(0, len(tiles), 2):
        vt = jnp.concatenate([vt_ref[p] for p in tiles[i:i + 2]], axis=1)
        e = jnp.concatenate([jnp.exp2((s - m).astype(BF16)) for s in scores[i:i + 2]], axis=0)
        pv = _dot(vt, e)
        acc = pv if acc is None else acc + pv
    return acc


def _normalized(acc):
    dh = HEAD_DIM
    return acc[:dh] * (1.0 / jnp.maximum(acc[dh:dh + 1], 1e-30))


def _attn_kernel(qt_ref, gt_ref, kc_ref, vct_ref, ks_ref, vst_ref, kw_ref, vwt_ref, cmap_ref,
                 slope_ref, o_ref, list_ref, ms_ref, as_ref, sbuf_ref, *, n_sel, nkt):
    hg, dh, tq, tk = HEADS_PER_GROUP, HEAD_DIM, Q_TILE, KEY_TILE
    qb = pl.program_id(2)
    t0 = qb * tq
    heads = range(hg)
    t_col = t0 + lax.broadcasted_iota(jnp.int32, (1, tq), 1)
    qt = qt_ref[...]
    slope_rows = slope_ref[...]
    q_pos = [jnp.concatenate([qt[h * dh:(h + 1) * dh, :], slope_rows[:, h * tq:(h + 1) * tq]], axis=0)
             for h in heads]

    nb = cmap_ref.shape[0]
    cur = t_col // SLC_LEN

    def compress_and_select(level):
        nb_v = LEVEL_BLOCKS * (level + 1)
        nc_v = nb_v * SLC_LEN // CMP_STRIDE
        cmp_end = lax.broadcasted_iota(jnp.int32, (nc_v, 1), 0) * CMP_STRIDE + (CMP_LEN - 1)
        m_c = cmp_end <= t_col
        kc, vct = kc_ref[0:nc_v, :], vct_ref[:, 0:nc_v]
        o_c, p_sum = [], None
        s_c_all = [_dot(kc, q_pos[h]) for h in heads]
        for h in heads:
            s_c = jnp.where(m_c, s_c_all[h], NEG_INF)
            m = jnp.maximum(jnp.max(s_c, axis=0, keepdims=True), SOFTMAX_FLOOR)
            p_c = jnp.exp2(s_c - m)
            p_c = p_c * (1.0 / jnp.maximum(jnp.sum(p_c, axis=0, keepdims=True), 1e-30))
            o_c.append(_dot(vct, p_c.astype(BF16)))
            p_sum = p_c if p_sum is None else p_sum + p_c

        cmap = cmap_ref[0:nb_v, 0:nc_v]
        imp = sum(_dot(cmap, part) for part in _split3(p_sum))
        blk = lax.broadcasted_iota(jnp.int32, (nb_v, tq), 0)
        forced = (blk == 0) | (blk == cur) | (blk == cur - 1)
        visible = blk <= cur
        score = jnp.where(visible, jnp.where(forced, -jnp.inf, imp), -1.0)
        blk_f = blk.astype(F32)
        for _ in range(n_sel - N_FORCED):
            best = jnp.max(score, axis=0, keepdims=True)
            first = jnp.min(jnp.where(score == best, blk_f, float(nb)), axis=0, keepdims=True)
            score = jnp.where(blk_f == first, -jnp.inf, score)
        sel = jnp.where(visible & (score == -jnp.inf), 1.0, 0.0)
        if nb_v < nb:
            sel = jnp.concatenate([sel, jnp.zeros((nb - nb_v, tq), F32)], axis=0)
        return jnp.concatenate(o_c, axis=0), sel

    n_levels = -(-(nkt * tk // SLC_LEN) // LEVEL_BLOCKS)
    level = (t0 + tq - 1) // (LEVEL_BLOCKS * SLC_LEN)
    o_c, sel = lax.switch(level, [functools.partial(compress_and_select, v) for v in range(n_levels)])
    o_c = [o_c[h * dh:(h + 1) * dh] for h in heads]

    cnt = _dot_nt(jnp.ones((8, tq), BF16), sel.astype(BF16))
    blk_any = jnp.where(cnt > 0.0, 1.0, 0.0).astype(BF16)
    r_i = lax.broadcasted_iota(jnp.int32, (nb, LANES), 0)
    c_i = lax.broadcasted_iota(jnp.int32, (nb, LANES), 1)
    bit = jnp.left_shift(1, r_i % FLAG_BITS).astype(F32)
    words = _dot(blk_any, jnp.where(r_i // FLAG_BITS == c_i, bit, 0.0).astype(BF16))
    words = [words[0, w].astype(jnp.int32) for w in range(nb // FLAG_BITS)]

    per_tile = tk // SLC_LEN
    n_w, n_q = WINDOW // tk, tq // tk
    n_before = t0 // tk
    n = jnp.int32(0)
    for p in range(nkt - n_q):
        word = words[p * per_tile // FLAG_BITS]
        hit = (jnp.right_shift(word, p * per_tile % FLAG_BITS) & (2 ** per_tile - 1)) != 0
        list_ref[n] = jnp.int32(p)
        n = n + (hit & (p < n_before)).astype(jnp.int32)
    for j in range(SEL_CHUNK):
        list_ref[n + j] = jnp.int32(nkt)

    key_row = lax.broadcasted_iota(jnp.int32, (tk, 1), 0)
    w_tiles, w_masks = [], []
    for j in range(n_w + n_q):
        p = n_before - n_w + j
        exists = p >= 0
        p = jnp.maximum(p, 0)
        w_tiles.append(p)
        if j < n_q:
            pos = jnp.where(exists, p * tk, -FAR_POSITION) + key_row
            w_masks.append(t_col - pos < WINDOW)
        elif j < n_w:
            w_masks.append(jnp.where(exists, 0.0, NEG_INF))
        else:
            w_masks.append(p * tk + key_row <= t_col)
    s_w = [_scores(kw_ref, w_tiles, q_pos[h], w_masks) for h in heads]

    bias = jnp.where(sel > 0.0, 0.0, -MASK_BIAS).astype(BF16)
    q_sel = [jnp.concatenate([bias, q_pos[h]], axis=0) for h in heads]
    d_tiles = [n_before + j for j in range(n_q)]
    d_masks = [p * tk + key_row <= t_col for p in d_tiles]
    s_d = [_scores(ks_ref, d_tiles, q_sel[h], d_masks) for h in heads]

    def chunk_tiles(c):
        return [list_ref[c * SEL_CHUNK + j] for j in range(SEL_CHUNK)]

    def chunk_scores(c, slot):
        tiles = chunk_tiles(c)
        for h in heads:
            for j, s in enumerate(_scores(ks_ref, tiles, q_sel[h], [None] * SEL_CHUNK)):
                sbuf_ref[slot, h, j] = s

    def chunk_update(c, slot):
        tiles = chunk_tiles(c)
        for h in heads:
            cols = slice(h * tq, (h + 1) * tq)
            s_s = [sbuf_ref[slot, h, j] for j in range(SEL_CHUNK)]
            m_prev = ms_ref[:, cols]
            m_new = _col_max(s_s, m_prev)
            ms_ref[:, cols] = m_new
            as_ref[:, cols] = (jnp.exp2(m_prev - m_new) * as_ref[:, cols]
                               + _weighted_values(vst_ref, tiles, s_s, m_new))

    chunk_scores(0, 0)
    o_w = [_normalized(_weighted_values(vwt_ref, w_tiles, s_w[h], _col_max(s_w[h]))) for h in heads]
    for h in heads:
        cols = slice(h * tq, (h + 1) * tq)
        m_d = _col_max(s_d[h])
        ms_ref[:, cols] = m_d
        as_ref[:, cols] = _weighted_values(vst_ref, d_tiles, s_d[h], m_d)

    def two_chunks(i, carry):
        chunk_scores(2 * i + 1, 1)
        chunk_update(2 * i, 0)
        chunk_scores(2 * i + 2, 0)
        chunk_update(2 * i + 1, 1)
        return carry

    n_chunks = jnp.maximum((n + SEL_CHUNK - 1) // SEL_CHUNK, 1)
    lax.fori_loop(0, (n_chunks - 1) // 2, two_chunks, 0)

    @pl.when(n_chunks % 2 == 0)
    def _():
        chunk_scores(n_chunks - 1, 1)
        chunk_update(n_chunks - 2, 0)
        chunk_update(n_chunks - 1, 1)

    @pl.when(n_chunks % 2 == 1)
    def _():
        chunk_update(n_chunks - 1, 0)

    gt = gt_ref[...]
    out = []
    for h in heads:
        r = 3 * h
        o_s = _normalized(as_ref[:, h * tq:(h + 1) * tq])
        o_h = gt[r:r + 1] * o_c[h] + gt[r + 1:r + 2] * o_s + gt[r + 2:r + 3] * o_w[h]
        out.append(o_h.T)
    o_ref[...] = jnp.concatenate(out, axis=1).astype(o_ref.dtype)


def _attend(qt, gates_t, kc, vct, ks, vst, kw, vwt, cmap, slopes):
    bsz, _, s = qt.shape
    hg, dh, tq = HEADS_PER_GROUP, HEAD_DIM, Q_TILE
    nchunk = kc.shape[2]
    n_sel = min(N_SELECT, s // SLC_LEN)
    nkt = s // KEY_TILE
    assert tq % KEY_TILE == 0 and tq <= WINDOW and n_sel >= N_FORCED
    per_bg = lambda *tail: pl.BlockSpec((None, None) + tail, lambda b, g, i: (b, g) + (0,) * len(tail))
    return pl.pallas_call(
        functools.partial(_attn_kernel, n_sel=n_sel, nkt=nkt),
        grid=(bsz, N_KV_GROUPS, s // tq),
        in_specs=[
            pl.BlockSpec((None, hg * dh, tq), lambda b, g, i: (b, g, i)),
            pl.BlockSpec((None, GATE_ROWS, tq), lambda b, g, i: (b, g, i)),
            per_bg(nchunk, dh + AUX_COLS), per_bg(dh, nchunk),
            per_bg(*ks.shape[2:]), per_bg(*vst.shape[2:]),
            per_bg(*kw.shape[2:]), per_bg(*vwt.shape[2:]),
            pl.BlockSpec(cmap.shape, lambda b, g, i: (0, 0)),
            pl.BlockSpec((None, AUX_COLS, hg * tq), lambda b, g, i: (g, 0, 0)),
        ],
        out_specs=pl.BlockSpec((None, tq, hg * dh), lambda b, g, i: (b, i, g)),
        out_shape=jax.ShapeDtypeStruct((bsz, s, N_HEADS * dh), BF16),
        scratch_shapes=[pltpu.SMEM((nkt + SEL_CHUNK,), jnp.int32),
                        pltpu.VMEM((1, hg * tq), F32), pltpu.VMEM((dh + V_PAD, hg * tq), F32),
                        pltpu.VMEM((2, hg, SEL_CHUNK, KEY_TILE, tq), F32)],
        compiler_params=_cparams("parallel", "parallel", "arbitrary"),
        name="nsa_attend",
    )(qt, gates_t, kc, vct, ks, vst, kw, vwt, cmap, slopes)


def _nsa_out_kernel(o_ref, wo_ref, x_ref, gm_ref, wu_ref, wd_ref, gf_ref, y_ref, *, final_norm):
    y = _mlp_rows(x_ref[...] + _dot(o_ref[...], wo_ref[...]), gm_ref, wu_ref, wd_ref)
    y_ref[...] = _rmsnorm(y, gf_ref[...]) if final_norm else y


def _nsa_out(o2, w_out, x2, g_mlp, w_up, w_down, g_final, final_norm):
    n, d = x2.shape
    weights = (g_mlp, w_up, w_down, g_final)
    return pl.pallas_call(
        functools.partial(_nsa_out_kernel, final_norm=final_norm),
        grid=(n // ROW_TILE,),
        in_specs=[pl.BlockSpec((ROW_TILE, o2.shape[1]), lambda i: (i, 0)), _resident(w_out.shape),
                  pl.BlockSpec((ROW_TILE, d), lambda i: (i, 0))]
        + [_resident(w.shape) for w in weights],
        out_specs=pl.BlockSpec((ROW_TILE, d), lambda i: (i, 0)),
        out_shape=jax.ShapeDtypeStruct((n, d), F32),
        compiler_params=_cparams("parallel"),
        name="nsa_out_mlp",
    )(o2, w_out, x2, *weights)


def _block_diag(w):
    eye = jnp.eye(CMP_GROUPS, dtype=w.dtype)
    out = jnp.einsum("gh,...ij->...gihj", eye, w)
    return out.reshape(*w.shape[:-2], CMP_GROUPS * w.shape[-2], CMP_GROUPS * w.shape[-1])


def _cmp_to_slc_map(nchunk, nb):
    cs = jnp.arange(nchunk) * CMP_STRIDE
    ss = jnp.arange(MAX_BLOCKS) * SLC_LEN
    ov = jnp.minimum(cs[None, :] + CMP_LEN, ss[:, None] + SLC_LEN) - jnp.maximum(cs[None, :], ss[:, None])
    ov = jnp.where(jnp.arange(MAX_BLOCKS)[:, None] < nb, ov, 0)
    return (jnp.maximum(ov, 0) / CMP_STRIDE).astype(BF16)


def _alibi_slope_rows():
    h = jnp.arange(1, N_HEADS + 1, dtype=F32)
    slope = jnp.exp2(-ALIBI_MAX * h / N_HEADS) * LOG2E
    hi = slope.astype(BF16)
    lo = (slope - hi.astype(F32)).astype(BF16)
    null = jnp.full_like(hi, -MASK_BIAS)
    rows = jnp.stack([hi, hi, lo, lo, null], axis=0)
    rows = jnp.pad(rows, ((0, AUX_COLS - rows.shape[0]), (0, 0)))
    rows = rows.reshape(AUX_COLS, N_KV_GROUPS, HEADS_PER_GROUP).transpose(1, 0, 2)
    return jnp.repeat(rows, Q_TILE, axis=2)


def _rg_layer(x, g, w_in, conv_w, conv_b, w_a, b_a, w_x, b_x, lam, w_out, g_mlp, w_up, w_down):
    row = lambda v: v.reshape(1, -1)
    return _rg_block(x, row(g), w_in.astype(BF16), conv_w, row(conv_b), w_a.astype(BF16), row(b_a),
                     w_x.astype(BF16), row(b_x), row(lam), w_out.astype(BF16),
                     row(g_mlp), w_up.astype(BF16), w_down.astype(BF16))


def _nsa_layer(x, g, w_in, b_gate, pe_k, pe_v, w1_k, w2_k, w1_v, w2_v, w_out,
               g_mlp, w_up, w_down, g_final, final_norm):
    bsz, s, d = x.shape
    assert s // SLC_LEN <= MAX_BLOCKS and s % ROW_TILE == 0
    q_cols = N_HEADS * HEAD_DIM
    kv_cols = N_KV_GROUPS * HEAD_DIM
    kv = w_in[:, q_cols:q_cols + 6 * kv_cols].reshape(d, 6, kv_cols)
    wq_t = w_in[:, :q_cols].T.astype(BF16)
    wc = kv[:, 0:2].reshape(d, 2 * kv_cols).astype(BF16)
    wk = jnp.concatenate([kv[:, 2], kv[:, 4]], axis=1).astype(BF16)
    wv_t = jnp.concatenate([kv[:, 3], kv[:, 5]], axis=1).T.astype(BF16)
    per_group = 3 * HEADS_PER_GROUP
    wg = w_in[:, q_cols + 6 * kv_cols:].reshape(d, N_KV_GROUPS, per_group)
    wg = jnp.pad(wg, ((0, 0), (0, 0), (0, GATE_ROWS - per_group))).reshape(d, -1)
    bg = jnp.pad(b_gate.reshape(N_KV_GROUPS, per_group), ((0, 0), (0, GATE_ROWS - per_group)))
    qt, kvc, ks, kw, vst, vwt, gates_t = _nsa_in(
        x, g.reshape(1, d), wq_t, wc, wk, wv_t, wg.T.astype(BF16), bg.reshape(-1, 1))

    pe = jnp.stack([jnp.tile(pe_k, (1, CMP_GROUPS)), jnp.tile(pe_v, (1, CMP_GROUPS))])
    w1 = jnp.stack([w1_k, w1_v]).reshape(2, CMP_LEN, HEAD_DIM, HEAD_DIM)
    w2 = jnp.stack([_block_diag(w2_k), _block_diag(w2_v).T])
    kc, vct = _compress(kvc, pe, _block_diag(w1).astype(BF16), w2.astype(BF16))

    cmap = _cmp_to_slc_map(s // CMP_STRIDE, s // SLC_LEN)
    o = _attend(qt, gates_t, kc, vct, ks, vst, kw, vwt, cmap, _alibi_slope_rows())
    y = _nsa_out(o.reshape(bsz * s, -1), w_out.astype(BF16), x.reshape(bsz * s, d),
                 g_mlp.reshape(1, d), w_up.astype(BF16), w_down.astype(BF16),
                 g_final.reshape(1, d), final_norm)
    return y.reshape(bsz, s, d)


def kernel(x, norm_mix, norm_ffn, norm_final, rg_w_in, rg_conv_w, rg_conv_b, rg_w_a, rg_b_a, rg_w_x, rg_b_x, rg_lambda, rg_w_out, nsa_w_in, nsa_b_gate, nsa_pe_k, nsa_pe_v, nsa_w1_k, nsa_w2_k, nsa_w1_v, nsa_w2_v, nsa_w_out, mlp_w_up, mlp_w_down):
    depth = norm_mix.shape[0]
    assert depth % 2 == 0
    for i in range(depth):
        j = i // 2
        if i % 2 == 0:
            x = _rg_layer(x, norm_mix[i], rg_w_in[j], rg_conv_w[j], rg_conv_b[j], rg_w_a[j],
                          rg_b_a[j], rg_w_x[j], rg_b_x[j], rg_lambda[j], rg_w_out[j],
                          norm_ffn[i], mlp_w_up[i], mlp_w_down[i])
        else:
            x = _nsa_layer(x, norm_mix[i], nsa_w_in[j], nsa_b_gate[j], nsa_pe_k[j], nsa_pe_v[j],
                           nsa_w1_k[j], nsa_w2_k[j], nsa_w1_v[j], nsa_w2_v[j], nsa_w_out[j],
                           norm_ffn[i], mlp_w_up[i], mlp_w_down[i], norm_final,
                           final_norm=(i == depth - 1))
    return x
```

```python
import functools
import math

import jax
import jax.numpy as jnp
from jax import lax
from jax.experimental import pallas as pl
from jax.experimental.pallas import tpu as pltpu

F32 = jnp.float32
BF16 = jnp.bfloat16

RMS_EPS = 1e-6
RG_BLOCKS = 4
CONV_W = 4
RG_C = 8.0
N_HEADS = 16
HEAD_DIM = 64
N_KV_GROUPS = 4
HEADS_PER_GROUP = N_HEADS // N_KV_GROUPS
CMP_LEN = 32
CMP_STRIDE = 16
SLC_LEN = 64
N_SELECT = 16
WINDOW = 512
ALIBI_MAX = 8.0
NEG_INF = -1e30
FORCE_BONUS = 1e4

VMEM_LIMIT_BYTES = 56 * 1024 * 1024
LANES = 128
SUBLANES = 8
CMP_GROUPS = LANES // HEAD_DIM

ROW_TILE = 512
SCAN_TILE = 256
Q_TILE = 256
KEY_TILE = 128
FF_CHUNK = 1024

MAX_BLOCKS = 128
AUX_COLS = 64
GATE_ROWS = 16
MASK_BIAS = 2.0 ** 100
SOFTMAX_FLOOR = -1e25
LEVEL_BLOCKS = 32
NULL_COL = 4
V_PAD = 16
LOG2E = math.log2(math.e)
FLAG_BITS = 16
N_FORCED = 3
SEL_CHUNK = 4
SEL_INLINE = 8
FAR_POSITION = 2 ** 30


def _cparams(*sem):
    return pltpu.CompilerParams(dimension_semantics=sem, vmem_limit_bytes=VMEM_LIMIT_BYTES)


def _resident(shape):
    nd = len(shape)
    return pl.BlockSpec(shape, lambda *_: (0,) * nd, pipeline_mode=pl.Buffered(1))


def _rmsnorm(x, g):
    y = x * lax.rsqrt(jnp.mean(x * x, axis=-1, keepdims=True) + RMS_EPS)
    return y * g


def _gelu_tanh(x):
    return jax.nn.gelu(x, approximate=True)


def _dot(a, b):
    return jnp.dot(a, b, preferred_element_type=F32)


def _dot_nt(a, b):
    return lax.dot_general(a, b, (((1,), (1,)), ((), ())), preferred_element_type=F32)


def _sigmoid(x):
    return 0.5 * jnp.tanh(0.5 * x) + 0.5


def _mlp_rows(x, g_ref, wu_ref, wd_ref):
    h = _rmsnorm(x, g_ref[...]).astype(BF16)
    acc = x
    for c in range(wu_ref.shape[1] // FF_CHUNK):
        cols = slice(c * FF_CHUNK, (c + 1) * FF_CHUNK)
        u = jnp.maximum(_dot(h, wu_ref[:, cols]), 0.0)
        acc = acc + _dot((u * u).astype(BF16), wd_ref[cols, :])
    return acc


def _rg_layer_kernel(x_ref, g_ref, win_ref, cw_ref, cb_ref, wa_ref, ba_ref, wx_ref, bx_ref,
                     lam_ref, wo_ref, gm_ref, wu_ref, wd_ref, o_ref, tail_ref, h_ref, mix_ref):
    ts, d = x_ref.shape
    bw = d // RG_BLOCKS

    @pl.when(pl.program_id(1) == 0)
    def _():
        tail_ref[...] = jnp.zeros_like(tail_ref)
        h_ref[...] = jnp.zeros_like(h_ref)
        mix_ref[...] = jnp.zeros_like(mix_ref)

    x = x_ref[...]
    proj = _dot(_rmsnorm(x, g_ref[...]).astype(BF16), win_ref[...])
    y = _gelu_tanh(proj[:, :d])
    xp = proj[:, d:]
    ext = jnp.concatenate([tail_ref[...], xp], axis=0)
    cw = cw_ref[...]
    xb = cb_ref[...] + cw[CONV_W - 1:CONV_W] * xp
    for k in range(1, CONV_W):
        xb = xb + cw[CONV_W - 1 - k:CONV_W - k] * pltpu.roll(ext, k, axis=0)[8:]
    tail_ref[...] = xp[ts - 8:]

    xb16 = xb.astype(BF16)
    ra = jnp.concatenate(
        [_dot(xb16[:, n * bw:(n + 1) * bw], wa_ref[n]) for n in range(RG_BLOCKS)], axis=1)
    rx = jnp.concatenate(
        [_dot(xb16[:, n * bw:(n + 1) * bw], wx_ref[n]) for n in range(RG_BLOCKS)], axis=1)

    o_ref[...] = _mlp_rows(mix_ref[...], gm_ref, wu_ref, wd_ref)

    r = _sigmoid(ra + ba_ref[...])
    i = _sigmoid(rx + bx_ref[...])
    nl = -lam_ref[...]
    softplus = jnp.maximum(nl, 0.0) + jnp.log1p(jnp.exp(-jnp.abs(nl)))
    log_a = -RG_C * r * softplus
    a = jnp.exp(log_a)
    b = jnp.sqrt(-jnp.tanh(log_a) * (a * a + 1.0)) * (i * xb)

    groups = ts // SUBLANES
    a = a.reshape(groups, SUBLANES, d)
    b = b.reshape(groups, SUBLANES, d)
    sub = lax.broadcasted_iota(jnp.int32, a.shape, 1)
    k = 1
    while k < SUBLANES:
        b = b + a * jnp.where(sub >= k, pltpu.roll(b, k, axis=1), 0.0)
        a = a * jnp.where(sub >= k, pltpu.roll(a, k, axis=1), 1.0)
        k *= 2
    carry = h_ref[0:1, :]
    rows = []
    for gi in range(groups):
        rows.append(a[gi] * carry + b[gi])
        carry = rows[-1][SUBLANES - 1:SUBLANES, :]
    hs = jnp.concatenate(rows, axis=0)
    h_ref[...] = jnp.broadcast_to(carry, h_ref.shape)

    mix_ref[...] = x + _dot((hs * y).astype(BF16), wo_ref[...])


def _rg_block(x, g, w_in, conv_w, conv_b, w_a, b_a, w_x, b_x, lam, w_out, g_mlp, w_up, w_down):
    bsz, s, d = x.shape
    n = s // SCAN_TILE
    weights = (g, w_in, conv_w, conv_b, w_a, b_a, w_x, b_x, lam, w_out, g_mlp, w_up, w_down)
    return pl.pallas_call(
        _rg_layer_kernel,
        grid=(bsz, n + 1),
        in_specs=[pl.BlockSpec((None, SCAN_TILE, d), lambda b, i: (b, jnp.minimum(i, n - 1), 0))]
        + [_resident(w.shape) for w in weights],
        out_specs=pl.BlockSpec((None, SCAN_TILE, d), lambda b, i: (b, jnp.maximum(i - 1, 0), 0)),
        out_shape=jax.ShapeDtypeStruct((bsz, s, d), F32),
        scratch_shapes=[pltpu.VMEM((8, d), F32), pltpu.VMEM((8, d), F32),
                        pltpu.VMEM((SCAN_TILE, d), F32)],
        compiler_params=_cparams("parallel", "arbitrary"),
        name="rg_block",
    )(x, *weights)


def _pos_terms(pos):
    lane = lax.broadcasted_iota(jnp.int32, (pos.shape[0], AUX_COLS), 1)
    hi = ((pos // SLC_LEN) * SLC_LEN).astype(F32)
    lo = (pos % SLC_LEN).astype(F32)
    return jnp.where(lane < 4, jnp.where(lane % 2 == 0, hi, lo), 0.0).astype(BF16)


def _nsa_in_kernel(x_ref, g_ref, wq_ref, wc_ref, wk_ref, wv_ref, wg_ref, bg_ref,
                   qt_ref, kvc_ref, ks_ref, kw_ref, vst_ref, vwt_ref, gt_ref):
    tm = x_ref.shape[0]
    dh = HEAD_DIM
    ng = ks_ref.shape[0]
    is_pad = pl.program_id(1) == pl.num_programs(1) - 1
    h = _rmsnorm(x_ref[...], g_ref[...]).astype(BF16)
    qt_ref[...] = (_dot_nt(wq_ref[...], h) * (dh ** -0.5 * LOG2E)).astype(qt_ref.dtype)
    kvc_ref[...] = _dot(h, wc_ref[...])
    gt_ref[...] = _sigmoid(_dot_nt(wg_ref[...], h) + bg_ref[...])

    row = jnp.minimum(pl.program_id(1), pl.num_programs(1) - 2) * tm
    pos = row + lax.broadcasted_iota(jnp.int32, (tm, 1), 0)
    terms = _pos_terms(pos)
    blk_lane = lax.broadcasted_iota(jnp.int32, (tm, MAX_BLOCKS), 1)
    onehot = jnp.where(blk_lane == pos // SLC_LEN, 1.0, 0.0).astype(BF16)
    null_lane = lax.broadcasted_iota(jnp.int32, (tm, MAX_BLOCKS + dh + AUX_COLS), 1)
    null_keys = jnp.where(null_lane == MAX_BLOCKS + dh + NULL_COL, 1.0, 0.0).astype(BF16)
    k = _dot(h, wk_ref[...]).astype(BF16)
    for gi in range(ng):
        ks_pos = jnp.concatenate([k[:, gi * dh:(gi + 1) * dh], terms], axis=1)
        ks_ref[gi] = jnp.where(is_pad, null_keys, jnp.concatenate([onehot, ks_pos], axis=1))
        kw_ref[gi] = jnp.concatenate([k[:, (ng + gi) * dh:(ng + gi + 1) * dh], terms], axis=1)
    vt = _dot_nt(wv_ref[...], h).astype(BF16)
    pad_row = lax.broadcasted_iota(jnp.int32, (V_PAD, KEY_TILE), 0)
    ones_rows = jnp.where(pad_row == 0, 1.0, 0.0).astype(BF16)
    for gi in range(ng):
        for c in range(tm // KEY_TILE):
            cols = slice(c * KEY_TILE, (c + 1) * KEY_TILE)
            vs_tile = jnp.concatenate([vt[gi * dh:(gi + 1) * dh, cols], ones_rows], axis=0)
            vst_ref[gi, c] = jnp.where(is_pad, jnp.zeros_like(vs_tile), vs_tile)
            vwt_ref[gi, c] = jnp.concatenate(
                [vt[(ng + gi) * dh:(ng + gi + 1) * dh, cols], ones_rows], axis=0)


def _nsa_in(x, g, wq_t, wc, wk, wv_t, wg_t, bg_t):
    bsz, s, d = x.shape
    ng, dh, kt = N_KV_GROUPS, HEAD_DIM, KEY_TILE
    per = ROW_TILE // kt
    n = s // ROW_TILE
    last = lambda i: jnp.minimum(i, n - 1)
    return pl.pallas_call(
        _nsa_in_kernel,
        grid=(bsz, n + 1),
        in_specs=[pl.BlockSpec((None, ROW_TILE, d), lambda b, i: (b, last(i), 0)),
                  _resident(g.shape), _resident(wq_t.shape), _resident(wc.shape),
                  _resident(wk.shape), _resident(wv_t.shape), _resident(wg_t.shape),
                  _resident(bg_t.shape)],
        out_specs=[
            pl.BlockSpec((None, wq_t.shape[0], ROW_TILE), lambda b, i: (b, 0, last(i))),
            pl.BlockSpec((None, ROW_TILE, wc.shape[1]), lambda b, i: (b, last(i), 0)),
            pl.BlockSpec((None, ng, ROW_TILE, dh + MAX_BLOCKS + AUX_COLS), lambda b, i: (b, 0, i, 0)),
            pl.BlockSpec((None, ng, ROW_TILE, dh + AUX_COLS), lambda b, i: (b, 0, last(i), 0)),
            pl.BlockSpec((None, ng, per, dh + V_PAD, kt), lambda b, i: (b, 0, i, 0, 0)),
            pl.BlockSpec((None, ng, per, dh + V_PAD, kt), lambda b, i: (b, 0, last(i), 0, 0)),
            pl.BlockSpec((None, wg_t.shape[0], ROW_TILE), lambda b, i: (b, 0, last(i))),
        ],
        out_shape=[
            jax.ShapeDtypeStruct((bsz, wq_t.shape[0], s), BF16),
            jax.ShapeDtypeStruct((bsz, s, wc.shape[1]), F32),
            jax.ShapeDtypeStruct((bsz, ng, s + ROW_TILE, dh + MAX_BLOCKS + AUX_COLS), BF16),
            jax.ShapeDtypeStruct((bsz, ng, s, dh + AUX_COLS), BF16),
            jax.ShapeDtypeStruct((bsz, ng, s // kt + per, dh + V_PAD, kt), BF16),
            jax.ShapeDtypeStruct((bsz, ng, s // kt, dh + V_PAD, kt), BF16),
            jax.ShapeDtypeStruct((bsz, wg_t.shape[0], s), F32),
        ],
        compiler_params=_cparams("parallel", "arbitrary"),
        name="nsa_in",
    )(x, g, wq_t, wc, wk, wv_t, wg_t, bg_t)


def _compress_kernel(zk_ref, zv_ref, pe_ref, w1_ref, w2_ref, kc_ref, vct_ref):
    nchunk = kc_ref.shape[1]
    dh = HEAD_DIM

    def pre_act(z_ref, t):
        acc = [jnp.zeros((nchunk, LANES), F32) for _ in range(CMP_LEN // CMP_STRIDE)]
        for p in range(CMP_STRIDE):
            zp = z_ref[pl.ds(p, nchunk, stride=CMP_STRIDE), :]
            for r in range(len(acc)):
                pp = r * CMP_STRIDE + p
                acc[r] = acc[r] + _dot((zp + pe_ref[t, pp:pp + 1, :]).astype(BF16), w1_ref[t, pp])
        return _gelu_tanh(acc[0] + pltpu.roll(acc[1], nchunk - 1, axis=0)).astype(BF16)

    kc = _dot(pre_act(zk_ref, 0), w2_ref[0]).astype(BF16)
    cmp_end = lax.broadcasted_iota(jnp.int32, (nchunk, 1), 0) * CMP_STRIDE + (CMP_LEN - 1)
    terms = _pos_terms(cmp_end)
    for gi in range(kc_ref.shape[0]):
        kc_ref[gi] = jnp.concatenate([kc[:, gi * dh:(gi + 1) * dh], terms], axis=1)
    vct = _dot_nt(w2_ref[1], pre_act(zv_ref, 1)).astype(BF16)
    for gi in range(vct_ref.shape[0]):
        vct_ref[gi] = vct[gi * dh:(gi + 1) * dh, :]


def _compress(kvc, pe, w1, w2):
    bsz, s, _ = kvc.shape
    nchunk = s // CMP_STRIDE
    halves = N_KV_GROUPS // CMP_GROUPS
    return pl.pallas_call(
        _compress_kernel,
        grid=(bsz, halves),
        in_specs=[
            pl.BlockSpec((None, s, LANES), lambda b, c: (b, 0, c)),
            pl.BlockSpec((None, s, LANES), lambda b, c: (b, 0, halves + c)),
            _resident(pe.shape), _resident(w1.shape), _resident(w2.shape),
        ],
        out_specs=[
            pl.BlockSpec((None, CMP_GROUPS, nchunk, HEAD_DIM + AUX_COLS), lambda b, c: (b, c, 0, 0)),
            pl.BlockSpec((None, CMP_GROUPS, HEAD_DIM, nchunk), lambda b, c: (b, c, 0, 0)),
        ],
        out_shape=[
            jax.ShapeDtypeStruct((bsz, N_KV_GROUPS, nchunk, HEAD_DIM + AUX_COLS), BF16),
            jax.ShapeDtypeStruct((bsz, N_KV_GROUPS, HEAD_DIM, nchunk), BF16),
        ],
        compiler_params=_cparams("parallel", "parallel"),
        name="nsa_compress",
    )(kvc, kvc, pe, w1, w2)


def _split3(x):
    hi = x.astype(BF16)
    r = x - hi.astype(F32)
    mid = r.astype(BF16)
    lo = (r - mid.astype(F32)).astype(BF16)
    return hi, mid, lo


def _key_tile(k_ref, p):
    return k_ref[pl.ds(pl.multiple_of(p * KEY_TILE, KEY_TILE), KEY_TILE), :]


def _col_max(scores, init=None):
    parts = [jnp.max(s, axis=0, keepdims=True) for s in scores]
    return functools.reduce(jnp.maximum, parts if init is None else [init] + parts)


def _scores(k_ref, tiles, q_aug, masks):
    out = []
    for p, mask in zip(tiles, masks):
        s = _dot(_key_tile(k_ref, p), q_aug)
        if mask is None:
            out.append(s)
        elif mask.dtype == jnp.bool_:
            out.append(jnp.where(mask, s, NEG_INF))
        else:
            out.append(s + mask)
    return out


def _weighted_values(vt_ref, tiles, scores, m, bf16_exponent=True):
    if bf16_exponent:
        exp2 = lambda x: jnp.exp2(x.astype(BF16))
    else:
        exp2 = lambda x: jnp.exp2(x).astype(BF16)
    acc = None
    for i in range(0, len(tiles), 2):
        vt = jnp.concatenate([vt_ref[p] for p in tiles[i:i + 2]], axis=1)
        e = jnp.concatenate([exp2(s - m) for s in scores[i:i + 2]], axis=0)
        pv = _dot(vt, e)
        acc = pv if acc is None else acc + pv
    return acc


def _normalized(acc):
    dh = HEAD_DIM
    return acc[:dh] * (1.0 / jnp.maximum(acc[dh:dh + 1], 1e-30))


def _attn_kernel(qt_ref, gt_ref, kc_ref, vct_ref, ks_ref, vst_ref, kw_ref, vwt_ref, cmap_ref,
                 slope_ref, o_ref, list_ref, ms_ref, as_ref, *, n_sel, nkt):
    hg, dh, tq, tk = HEADS_PER_GROUP, HEAD_DIM, Q_TILE, KEY_TILE
    qb = pl.program_id(2)
    t0 = qb * tq
    heads = range(hg)
    t_col = t0 + lax.broadcasted_iota(jnp.int32, (1, tq), 1)
    qt = qt_ref[...]
    slope_rows = slope_ref[...]
    q_pos = [jnp.concatenate([qt[h * dh:(h + 1) * dh, :], slope_rows[:, h * tq:(h + 1) * tq]], axis=0)
             for h in heads]

    nb = cmap_ref.shape[0]
    cur = t_col // SLC_LEN

    def compress_and_select(level):
        nb_v = LEVEL_BLOCKS * (level + 1)
        nc_v = nb_v * SLC_LEN // CMP_STRIDE
        cmp_end = lax.broadcasted_iota(jnp.int32, (nc_v, 1), 0) * CMP_STRIDE + (CMP_LEN - 1)
        m_c = cmp_end <= t_col
        kc, vct = kc_ref[0:nc_v, :], vct_ref[:, 0:nc_v]
        o_c, p_sum = [], None
        s_c_all = [_dot(kc, q_pos[h]) for h in heads]
        for h in heads:
            s_c = jnp.where(m_c, s_c_all[h], NEG_INF)
            m = jnp.maximum(jnp.max(s_c, axis=0, keepdims=True), SOFTMAX_FLOOR)
            p_c = jnp.exp2(s_c - m)
            p_c = p_c * (1.0 / jnp.maximum(jnp.sum(p_c, axis=0, keepdims=True), 1e-30))
            o_c.append(_dot(vct, p_c.astype(BF16)))
            p_sum = p_c if p_sum is None else p_sum + p_c

        cmap = cmap_ref[0:nb_v, 0:nc_v]
        imp = sum(_dot(cmap, part) for part in _split3(p_sum))
        blk = lax.broadcasted_iota(jnp.int32, (nb_v, tq), 0)
        forced = (blk == 0) | (blk == cur) | (blk == cur - 1)
        visible = blk <= cur
        score = jnp.where(visible, jnp.where(forced, -jnp.inf, imp), -1.0)
        blk_f = blk.astype(F32)
        for _ in range(n_sel - N_FORCED):
            best = jnp.max(score, axis=0, keepdims=True)
            first = jnp.min(jnp.where(score == best, blk_f, float(nb)), axis=0, keepdims=True)
            score = jnp.where(blk_f == first, -jnp.inf, score)
        sel = jnp.where(visible & (score == -jnp.inf), 1.0, 0.0)
        if nb_v < nb:
            sel = jnp.concatenate([sel, jnp.zeros((nb - nb_v, tq), F32)], axis=0)
        return jnp.concatenate(o_c, axis=0), sel

    n_levels = -(-(nkt * tk // SLC_LEN) // LEVEL_BLOCKS)
    level = (t0 + tq - 1) // (LEVEL_BLOCKS * SLC_LEN)
    o_c, sel = lax.switch(level, [functools.partial(compress_and_select, v) for v in range(n_levels)])
    o_c = [o_c[h * dh:(h + 1) * dh] for h in heads]

    cnt = _dot_nt(jnp.ones((8, tq), BF16), sel.astype(BF16))
    blk_any = jnp.where(cnt > 0.0, 1.0, 0.0).astype(BF16)
    r_i = lax.broadcasted_iota(jnp.int32, (nb, LANES), 0)
    c_i = lax.broadcasted_iota(jnp.int32, (nb, LANES), 1)
    bit = jnp.left_shift(1, r_i % FLAG_BITS).astype(F32)
    words = _dot(blk_any, jnp.where(r_i // FLAG_BITS == c_i, bit, 0.0).astype(BF16))
    words = [words[0, w].astype(jnp.int32) for w in range(nb // FLAG_BITS)]

    per_tile = tk // SLC_LEN
    n_w, n_q = WINDOW // tk, tq // tk
    n_before = t0 // tk
    n = jnp.int32(0)
    for p in range(nkt - n_q):
        word = words[p * per_tile // FLAG_BITS]
        hit = (jnp.right_shift(word, p * per_tile % FLAG_BITS) & (2 ** per_tile - 1)) != 0
        list_ref[n] = jnp.int32(p)
        n = n + (hit & (p < n_before)).astype(jnp.int32)
    for j in range(SEL_INLINE):
        list_ref[n + j] = jnp.int32(nkt)

    key_row = lax.broadcasted_iota(jnp.int32, (tk, 1), 0)
    w_tiles, w_masks = [], []
    for j in range(n_w + n_q):
        p = n_before - n_w + j
        exists = p >= 0
        p = jnp.maximum(p, 0)
        w_tiles.append(p)
        if j < n_q:
            pos = jnp.where(exists, p * tk, -FAR_POSITION) + key_row
            w_masks.append(t_col - pos < WINDOW)
        elif j < n_w:
            w_masks.append(jnp.where(exists, 0.0, NEG_INF))
        else:
            w_masks.append(p * tk + key_row <= t_col)
    s_w = [_scores(kw_ref, w_tiles, q_pos[h], w_masks) for h in heads]

    bias = jnp.where(sel > 0.0, 0.0, -MASK_BIAS).astype(BF16)
    q_sel = [jnp.concatenate([bias, q_pos[h]], axis=0) for h in heads]
    d_tiles = [n_before + j for j in range(n_q)]
    d_masks = [p * tk + key_row <= t_col for p in d_tiles]
    s_d = [_scores(ks_ref, d_tiles, q_sel[h], d_masks) for h in heads]

    o_w = [_normalized(_weighted_values(vwt_ref, w_tiles, s_w[h], _col_max(s_w[h]))) for h in heads]

    def diagonal_tiles(s_d):
        for h in heads:
            cols = slice(h * tq, (h + 1) * tq)
            m_d = _col_max(s_d[h])
            ms_ref[:, cols] = m_d
            as_ref[:, cols] = _weighted_values(vst_ref, d_tiles, s_d[h], m_d)

    def chunk_scores(c, size=SEL_CHUNK):
        tiles = [list_ref[c * SEL_CHUNK + j] for j in range(size)]
        return tiles, [_scores(ks_ref, tiles, q_sel[h], [None] * size) for h in heads]

    def stream_chunk(c, carry, size=SEL_CHUNK):
        tiles, s_s = chunk_scores(c, size)
        for h in heads:
            cols = slice(h * tq, (h + 1) * tq)
            as_ref[:, cols] += _weighted_values(vst_ref, tiles, s_s[h], ms_ref[:, cols],
                                                bf16_exponent=False)
        return carry

    def online_chunk(c, carry):
        tiles, s_s = chunk_scores(c)
        for h in heads:
            cols = slice(h * tq, (h + 1) * tq)
            m_prev = ms_ref[:, cols]
            m_new = _col_max(s_s[h], m_prev)
            ms_ref[:, cols] = m_new
            as_ref[:, cols] = (jnp.exp2(m_prev - m_new) * as_ref[:, cols]
                               + _weighted_values(vst_ref, tiles, s_s[h], m_new))
        return carry

    n_chunks = (n + SEL_CHUNK - 1) // SEL_CHUNK
    diagonal_tiles(s_d)
    stream_chunk(0, 0, SEL_INLINE)
    lax.fori_loop(SEL_INLINE // SEL_CHUNK, n_chunks, stream_chunk, 0)

    @pl.when(jnp.logical_not(jnp.max(jnp.abs(as_ref[...])) < jnp.inf))
    def _():
        diagonal_tiles([_scores(ks_ref, d_tiles, q_sel[h], d_masks) for h in heads])
        lax.fori_loop(0, n_chunks, online_chunk, 0)

    gt = gt_ref[...]
    out = []
    for h in heads:
        r = 3 * h
        o_s = _normalized(as_ref[:, h * tq:(h + 1) * tq])
        o_h = gt[r:r + 1] * o_c[h] + gt[r + 1:r + 2] * o_s + gt[r + 2:r + 3] * o_w[h]
        out.append(o_h.T)
    o_ref[...] = jnp.concatenate(out, axis=1).astype(o_ref.dtype)


def _attend(qt, gates_t, kc, vct, ks, vst, kw, vwt, cmap, slopes):
    bsz, _, s = qt.shape
    hg, dh, tq = HEADS_PER_GROUP, HEAD_DIM, Q_TILE
    nchunk = kc.shape[2]
    n_sel = min(N_SELECT, s // SLC_LEN)
    nkt = s // KEY_TILE
    assert tq % KEY_TILE == 0 and tq <= WINDOW and n_sel >= N_FORCED
    per_bg = lambda *tail: pl.BlockSpec((None, None) + tail, lambda b, g, i: (b, g) + (0,) * len(tail))
    return pl.pallas_call(
        functools.partial(_attn_kernel, n_sel=n_sel, nkt=nkt),
        grid=(bsz, N_KV_GROUPS, s // tq),
        in_specs=[
            pl.BlockSpec((None, hg * dh, tq), lambda b, g, i: (b, g, i)),
            pl.BlockSpec((None, GATE_ROWS, tq), lambda b, g, i: (b, g, i)),
            per_bg(nchunk, dh + AUX_COLS), per_bg(dh, nchunk),
            per_bg(*ks.shape[2:]), per_bg(*vst.shape[2:]),
            per_bg(*kw.shape[2:]), per_bg(*vwt.shape[2:]),
            pl.BlockSpec(cmap.shape, lambda b, g, i: (0, 0)),
            pl.BlockSpec((None, AUX_COLS, hg * tq), lambda b, g, i: (g, 0, 0)),
        ],
        out_specs=pl.BlockSpec((None, tq, hg * dh), lambda b, g, i: (b, i, g)),
        out_shape=jax.ShapeDtypeStruct((bsz, s, N_HEADS * dh), BF16),
        scratch_shapes=[pltpu.SMEM((nkt + SEL_INLINE,), jnp.int32),
                        pltpu.VMEM((1, hg * tq), F32), pltpu.VMEM((dh + V_PAD, hg * tq), F32)],
        compiler_params=_cparams("parallel", "parallel", "arbitrary"),
        name="nsa_attend",
    )(qt, gates_t, kc, vct, ks, vst, kw, vwt, cmap, slopes)


def _nsa_out_kernel(o_ref, wo_ref, x_ref, gm_ref, wu_ref, wd_ref, gf_ref, y_ref, *, final_norm):
    y = _mlp_rows(x_ref[...] + _dot(o_ref[...], wo_ref[...]), gm_ref, wu_ref, wd_ref)
    y_ref[...] = _rmsnorm(y, gf_ref[...]) if final_norm else y


def _nsa_out(o2, w_out, x2, g_mlp, w_up, w_down, g_final, final_norm):
    n, d = x2.shape
    weights = (g_mlp, w_up, w_down, g_final)
    return pl.pallas_call(
        functools.partial(_nsa_out_kernel, final_norm=final_norm),
        grid=(n // ROW_TILE,),
        in_specs=[pl.BlockSpec((ROW_TILE, o2.shape[1]), lambda i: (i, 0)), _resident(w_out.shape),
                  pl.BlockSpec((ROW_TILE, d), lambda i: (i, 0))]
        + [_resident(w.shape) for w in weights],
        out_specs=pl.BlockSpec((ROW_TILE, d), lambda i: (i, 0)),
        out_shape=jax.ShapeDtypeStruct((n, d), F32),
        compiler_params=_cparams("parallel"),
        name="nsa_out_mlp",
    )(o2, w_out, x2, *weights)


def _block_diag(w):
    eye = jnp.eye(CMP_GROUPS, dtype=w.dtype)
    out = jnp.einsum("gh,...ij->...gihj", eye, w)
    return out.reshape(*w.shape[:-2], CMP_GROUPS * w.shape[-2], CMP_GROUPS * w.shape[-1])


def _cmp_to_slc_map(nchunk, nb):
    cs = jnp.arange(nchunk) * CMP_STRIDE
    ss = jnp.arange(MAX_BLOCKS) * SLC_LEN
    ov = jnp.minimum(cs[None, :] + CMP_LEN, ss[:, None] + SLC_LEN) - jnp.maximum(cs[None, :], ss[:, None])
    ov = jnp.where(jnp.arange(MAX_BLOCKS)[:, None] < nb, ov, 0)
    return (jnp.maximum(ov, 0) / CMP_STRIDE).astype(BF16)


def _alibi_slope_rows():
    h = jnp.arange(1, N_HEADS + 1, dtype=F32)
    slope = jnp.exp2(-ALIBI_MAX * h / N_HEADS) * LOG2E
    hi = slope.astype(BF16)
    lo = (slope - hi.astype(F32)).astype(BF16)
    null = jnp.full_like(hi, -MASK_BIAS)
    rows = jnp.stack([hi, hi, lo, lo, null], axis=0)
    rows = jnp.pad(rows, ((0, AUX_COLS - rows.shape[0]), (0, 0)))
    rows = rows.reshape(AUX_COLS, N_KV_GROUPS, HEADS_PER_GROUP).transpose(1, 0, 2)
    return jnp.repeat(rows, Q_TILE, axis=2)


def _rg_layer(x, g, w_in, conv_w, conv_b, w_a, b_a, w_x, b_x, lam, w_out, g_mlp, w_up, w_down):
    row = lambda v: v.reshape(1, -1)
    return _rg_block(x, row(g), w_in.astype(BF16), conv_w, row(conv_b), w_a.astype(BF16), row(b_a),
                     w_x.astype(BF16), row(b_x), row(lam), w_out.astype(BF16),
                     row(g_mlp), w_up.astype(BF16), w_down.astype(BF16))


def _nsa_layer(x, g, w_in, b_gate, pe_k, pe_v, w1_k, w2_k, w1_v, w2_v, w_out,
               g_mlp, w_up, w_down, g_final, final_norm):
    bsz, s, d = x.shape
    assert s // SLC_LEN <= MAX_BLOCKS and s % ROW_TILE == 0
    q_cols = N_HEADS * HEAD_DIM
    kv_cols = N_KV_GROUPS * HEAD_DIM
    kv = w_in[:, q_cols:q_cols + 6 * kv_cols].reshape(d, 6, kv_cols)
    wq_t = w_in[:, :q_cols].T.astype(BF16)
    wc = kv[:, 0:2].reshape(d, 2 * kv_cols).astype(BF16)
    wk = jnp.concatenate([kv[:, 2], kv[:, 4]], axis=1).astype(BF16)
    wv_t = jnp.concatenate([kv[:, 3], kv[:, 5]], axis=1).T.astype(BF16)
    per_group = 3 * HEADS_PER_GROUP
    wg = w_in[:, q_cols + 6 * kv_cols:].reshape(d, N_KV_GROUPS, per_group)
    wg = jnp.pad(wg, ((0, 0), (0, 0), (0, GATE_ROWS - per_group))).reshape(d, -1)
    bg = jnp.pad(b_gate.reshape(N_KV_GROUPS, per_group), ((0, 0), (0, GATE_ROWS - per_group)))
    qt, kvc, ks, kw, vst, vwt, gates_t = _nsa_in(
        x, g.reshape(1, d), wq_t, wc, wk, wv_t, wg.T.astype(BF16), bg.reshape(-1, 1))

    pe = jnp.stack([jnp.tile(pe_k, (1, CMP_GROUPS)), jnp.tile(pe_v, (1, CMP_GROUPS))])
    w1 = jnp.stack([w1_k, w1_v]).reshape(2, CMP_LEN, HEAD_DIM, HEAD_DIM)
    w2 = jnp.stack([_block_diag(w2_k), _block_diag(w2_v).T])
    kc, vct = _compress(kvc, pe, _block_diag(w1).astype(BF16), w2.astype(BF16))

    cmap = _cmp_to_slc_map(s // CMP_STRIDE, s // SLC_LEN)
    o = _attend(qt, gates_t, kc, vct, ks, vst, kw, vwt, cmap, _alibi_slope_rows())
    y = _nsa_out(o.reshape(bsz * s, -1), w_out.astype(BF16), x.reshape(bsz * s, d),
                 g_mlp.reshape(1, d), w_up.astype(BF16), w_down.astype(BF16),
                 g_final.reshape(1, d), final_norm)
    return y.reshape(bsz, s, d)


def kernel(x, norm_mix, norm_ffn, norm_final, rg_w_in, rg_conv_w, rg_conv_b, rg_w_a, rg_b_a, rg_w_x, rg_b_x, rg_lambda, rg_w_out, nsa_w_in, nsa_b_gate, nsa_pe_k, nsa_pe_v, nsa_w1_k, nsa_w2_k, nsa_w1_v, nsa_w2_v, nsa_w_out, mlp_w_up, mlp_w_down):
    depth = norm_mix.shape[0]
    assert depth % 2 == 0
    for i in range(depth):
        j = i // 2
        if i % 2 == 0:
            x = _rg_layer(x, norm_mix[i], rg_w_in[j], rg_conv_w[j], rg_conv_b[j], rg_w_a[j],
                          rg_b_a[j], rg_w_x[j], rg_b_x[j], rg_lambda[j], rg_w_out[j],
                          norm_ffn[i], mlp_w_up[i], mlp_w_down[i])
        else:
            x = _nsa_layer(x, norm_mix[i], nsa_w_in[j], nsa_b_gate[j], nsa_pe_k[j], nsa_pe_v[j],
                           nsa_w1_k[j], nsa_w2_k[j], nsa_w1_v[j], nsa_w2_v[j], nsa_w_out[j],
                           norm_ffn[i], mlp_w_up[i], mlp_w_down[i], norm_final,
                           final_norm=(i == depth - 1))
    return x
```
